```python
import math
import jax
import jax.numpy as jnp
from jax import lax
import numpy as np

D_MODEL = 1024
BATCH = 8
SEQ = 2048
DEPTH = 4
DEC_BATCH = 128
DEC_SEQ = 8
PAST_LEN = 8192
PAGE_SIZE = 128

D_A = D_MODEL // 2
CONV_A_WIDTH = 31
D_B = D_MODEL
SSM_HEAD_DIM = 64
SSM_HEADS = D_B // SSM_HEAD_DIM
SSM_GROUPS = 2
SSM_STATE = 128
CONV_B_WIDTH = 4
SSD_CHUNK = 128
CONV_B_DIM = D_B + 2 * SSM_GROUPS * SSM_STATE
MLA_HEADS = D_MODEL // 128
NOPE_DIM = 64
ROPE_DIM = 32
QK_DIM = NOPE_DIM + ROPE_DIM
V_DIM = 64
D_C = MLA_HEADS * V_DIM
Q_LORA = 3 * D_MODEL // 8
KV_LORA = D_MODEL // 4
ROPE_BASE = 10000.0
Q_BLOCK = 128
N_BRANCH = 3
EPS = 1e-6

IN_SPLITS = (D_A, D_A, D_A, D_B, D_B, SSM_GROUPS * SSM_STATE, SSM_GROUPS * SSM_STATE, SSM_HEADS,
             Q_LORA, KV_LORA, ROPE_DIM, D_C, N_BRANCH * D_MODEL)
IN_DIM = sum(IN_SPLITS)

kernel_name = 'hybrid_conformer_ssd_mla_step'


def rmsnorm(x, w):
    xf = x.astype(jnp.float32)
    xf = xf * lax.rsqrt(jnp.mean(xf * xf, axis=-1, keepdims=True) + EPS)
    return xf.astype(x.dtype) * w


def layernorm(x, w, b):
    xf = x.astype(jnp.float32)
    xc = xf - jnp.mean(xf, axis=-1, keepdims=True)
    xf = xc * lax.rsqrt(jnp.mean(xc * xc, axis=-1, keepdims=True) + EPS)
    return xf.astype(x.dtype) * w + b


def causal_dwconv(hist, u, w, b):
    full = jnp.concatenate([hist.astype(u.dtype), u], axis=1)
    out = lax.conv_general_dilated(full, w[:, None, :].astype(u.dtype), window_strides=(1,), padding='VALID',
                                   dimension_numbers=('NWC', 'WIO', 'NWC'), feature_group_count=u.shape[-1])
    return out + b, full[:, full.shape[1] - (w.shape[0] - 1):]


def rope(x, pos):
    half = ROPE_DIM // 2
    inv = jnp.power(ROPE_BASE, -jnp.arange(half, dtype=jnp.float32) / half)
    ang = pos.astype(jnp.float32)[:, None] * inv[None, :]
    cos = jnp.cos(ang)[:, None, :].astype(x.dtype)
    sin = jnp.sin(ang)[:, None, :].astype(x.dtype)
    x1, x2 = x[..., :half], x[..., half:]
    return jnp.concatenate([x1 * cos - x2 * sin, x2 * cos + x1 * sin], axis=-1)


def norm_rope_heads(v, pos, w):
    v = rmsnorm(v, w)
    return jnp.concatenate([v[..., :NOPE_DIM], rope(v[..., NOPE_DIM:], pos)], axis=-1)


def mla_sequence(q, lat, kr, q_pos, k_pos, p):
    tq, tk = q.shape[0], lat.shape[0]
    q = norm_rope_heads(q, q_pos, p['q_norm_w'])
    k_nope = jnp.einsum('kc,chd->khd', lat, p['w_uk'])
    k = jnp.concatenate([k_nope, jnp.broadcast_to(kr[:, None, :], (tk, MLA_HEADS, ROPE_DIM))], axis=-1)
    k = norm_rope_heads(k, k_pos, p['k_norm_w'])
    qb = min(Q_BLOCK, tq)
    nb = -(-tq // qb)
    pad = nb * qb - tq
    qp = q_pos
    if pad:
        q = jnp.pad(q, ((0, pad), (0, 0), (0, 0)))
        qp = jnp.concatenate([q_pos, jnp.full((pad,), q_pos[-1], q_pos.dtype)])
    scale = QK_DIM ** -0.5
    w_uv = p['w_uv']

    def block(args):
        qq, pp = args
        s = jnp.einsum('qhd,khd->hqk', qq, k).astype(jnp.float32) * scale
        s = jnp.where(k_pos[None, None, :] <= pp[None, :, None], s, -jnp.inf)
        pr = jax.nn.softmax(s, axis=-1).astype(lat.dtype)
        ctx = jnp.einsum('hqk,kc->qhc', pr, lat)
        return jnp.einsum('qhc,chd->qhd', ctx, w_uv)

    out = lax.map(block, (q.reshape(nb, qb, MLA_HEADS, QK_DIM), qp.reshape(nb, qb)))
    return out.reshape(nb * qb, MLA_HEADS, V_DIM)[:tq]


def ssd_scan(x, dt, a, bm, cm, h0):
    b, t = x.shape[0], x.shape[1]
    q = min(SSD_CHUNK, t)
    pad = (-t) % q
    x = x.astype(jnp.float32)
    bm = bm.astype(jnp.float32)
    cm = cm.astype(jnp.float32)
    if pad:
        x = jnp.pad(x, ((0, 0), (0, pad), (0, 0), (0, 0)))
        dt = jnp.pad(dt, ((0, 0), (0, pad), (0, 0)))
        bm = jnp.pad(bm, ((0, 0), (0, pad), (0, 0), (0, 0)))
        cm = jnp.pad(cm, ((0, 0), (0, pad), (0, 0), (0, 0)))
    nc = (t + pad) // q
    hg = SSM_HEADS // SSM_GROUPS
    x = x.reshape(b, nc, q, SSM_GROUPS, hg, SSM_HEAD_DIM)
    dt = dt.reshape(b, nc, q, SSM_GROUPS, hg)
    bm = bm.reshape(b, nc, q, SSM_GROUPS, SSM_STATE)
    cm = cm.reshape(b, nc, q, SSM_GROUPS, SSM_STATE)
    acum = jnp.cumsum(dt * a.reshape(SSM_GROUPS, hg), axis=2)
    seg = acum[:, :, :, None] - acum[:, :, None, :]
    mask = jnp.tril(jnp.ones((q, q), bool))[None, None, :, :, None, None]
    lmat = jnp.exp(jnp.where(mask, seg, -jnp.inf))
    cb = jnp.einsum('bcign,bcjgn->bcijg', cm, bm)
    wmat = cb[..., None] * lmat * dt[:, :, None]
    y_diag = jnp.einsum('bcijgh,bcjghp->bcighp', wmat, x)
    decay_end = jnp.exp(acum[:, :, -1:] - acum)
    states = jnp.einsum('bcjgn,bcjgh,bcjghp->bcghpn', bm, decay_end * dt, x)
    chunk_decay = jnp.exp(acum[:, :, -1])

    def step(hc, inp):
        dec, st = inp
        return dec[..., None, None] * hc + st, hc

    h_init = h0.reshape(b, SSM_GROUPS, hg, SSM_HEAD_DIM, SSM_STATE).astype(jnp.float32)
    h_fin, h_in = lax.scan(step, h_init, (jnp.moveaxis(chunk_decay, 1, 0), jnp.moveaxis(states, 1, 0)))
    h_in = jnp.moveaxis(h_in, 0, 1)
    y_off = jnp.einsum('bcign,bcghpn,bcigh->bcighp', cm, h_in, jnp.exp(acum))
    y = (y_diag + y_off).reshape(b, nc * q, SSM_HEADS, SSM_HEAD_DIM)[:, :t]
    return y, h_fin.reshape(b, SSM_HEADS, SSM_HEAD_DIM, SSM_STATE).astype(h0.dtype)


def mixer_layer(x, hist_a, hist_b, h0, attend, p):
    b, t, _ = x.shape
    gn = SSM_GROUPS * SSM_STATE
    h = rmsnorm(x, p['norm_w'])
    u = h @ p['w_in']
    split_points = [int(s) for s in np.cumsum(IN_SPLITS)[:-1]]
    (a_val, a_gate, a_silu, z, xb, bm, cm, dt, cq, ckv, kr, c_silu, mg) = jnp.split(u, split_points, axis=-1)
    glu = a_val * jax.nn.sigmoid(a_gate)
    ca, new_a = causal_dwconv(hist_a, glu, p['conv_a_w'], p['conv_a_b'])
    ya = jax.nn.silu(layernorm(ca, p['ln_a_w'], p['ln_a_b'])) * jax.nn.silu(a_silu)
    br_a = ya @ p['w_a_out']
    xbc, new_b = causal_dwconv(hist_b, jnp.concatenate([xb, bm, cm], axis=-1), p['conv_b_w'], p['conv_b_b'])
    xbc = jax.nn.silu(xbc)
    xs, bs, cs = jnp.split(xbc, [D_B, D_B + gn], axis=-1)
    xs = xs.reshape(b, t, SSM_HEADS, SSM_HEAD_DIM)
    dtv = jax.nn.softplus((dt + p['dt_bias']).astype(jnp.float32))
    a = -jnp.exp(p['a_log'].astype(jnp.float32))
    y, h_fin = ssd_scan(xs, dtv, a, bs.reshape(b, t, SSM_GROUPS, SSM_STATE), cs.reshape(b, t, SSM_GROUPS, SSM_STATE), h0)
    y = (y + p['d_skip'].astype(jnp.float32)[:, None] * xs.astype(jnp.float32)).astype(x.dtype)
    yb = rmsnorm(y.reshape(b, t, D_B) * jax.nn.silu(z), p['norm_b_w'])
    br_b = yb @ p['w_b_out']
    q = (rmsnorm(cq, p['q_a_norm_w']) @ p['w_uq']).reshape(b, t, MLA_HEADS, QK_DIM)
    lat = rmsnorm(ckv, p['kv_a_norm_w'])
    att = attend(q, lat, kr, p)
    br_c = (att.reshape(b, t, D_C) * jax.nn.silu(c_silu)) @ p['w_c_out']
    g = jax.nn.sigmoid(mg).reshape(b, t, N_BRANCH, D_MODEL)
    merged = g[:, :, 0] * br_a + g[:, :, 1] * br_b + g[:, :, 2] * br_c
    return x + merged @ p['w_out'], new_a, new_b, h_fin, lat, kr


def setup_inputs(seed: int = 0) -> dict:
    key = jax.random.key(seed)
    ks = iter(jax.random.split(key, 40))

    def nrm(shape, scale):
        return jax.random.normal(next(ks), shape, jnp.float32) * scale

    def gain(shape):
        return 1.0 + nrm(shape, 0.02)

    n_pages = PAST_LEN // PAGE_SIZE
    n_used = DEC_BATCH * n_pages
    n_pool = n_used + max(1, n_used // 4)
    x_prompt = nrm((BATCH, SEQ, D_MODEL), 1.0)
    x_sample = nrm((DEC_BATCH, DEC_SEQ, D_MODEL), 1.0)
    state_conv_a = nrm((DEPTH, DEC_BATCH, CONV_A_WIDTH - 1, D_A), 0.5)
    state_conv_b = nrm((DEPTH, DEC_BATCH, CONV_B_WIDTH - 1, CONV_B_DIM), 0.5)
    state_ssm = nrm((DEPTH, DEC_BATCH, SSM_HEADS, SSM_HEAD_DIM, SSM_STATE), 0.1)
    cache_latent = nrm((DEPTH, n_pool, PAGE_SIZE, KV_LORA), 1.0)
    cache_krope = nrm((DEPTH, n_pool, PAGE_SIZE, ROPE_DIM), 1.0)
    page_table = jax.random.permutation(next(ks), n_pool)[:n_used].reshape(DEC_BATCH, n_pages).astype(jnp.int32)
    dt0 = jnp.exp(jax.random.uniform(next(ks), (DEPTH, SSM_HEADS), jnp.float32, math.log(1e-3), math.log(1e-1)))
    dt_bias = dt0 + jnp.log(-jnp.expm1(-dt0))
    a_log = jnp.log(jax.random.uniform(next(ks), (DEPTH, SSM_HEADS), jnp.float32, 1.0, 16.0))
    return {
        'x_prompt': x_prompt,
        'x_sample': x_sample,
        'state_conv_a': state_conv_a,
        'state_conv_b': state_conv_b,
        'state_ssm': state_ssm,
        'cache_latent': cache_latent,
        'cache_krope': cache_krope,
        'page_table': page_table,
        'norm_w': gain((DEPTH, D_MODEL)),
        'w_in': nrm((DEPTH, D_MODEL, IN_DIM), D_MODEL ** -0.5),
        'conv_a_w': nrm((DEPTH, CONV_A_WIDTH, D_A), CONV_A_WIDTH ** -0.5),
        'conv_a_b': nrm((DEPTH, D_A), 0.02),
        'ln_a_w': gain((DEPTH, D_A)),
        'ln_a_b': nrm((DEPTH, D_A), 0.02),
        'w_a_out': nrm((DEPTH, D_A, D_MODEL), D_A ** -0.5),
        'conv_b_w': nrm((DEPTH, CONV_B_WIDTH, CONV_B_DIM), CONV_B_WIDTH ** -0.5),
        'conv_b_b': nrm((DEPTH, CONV_B_DIM), 0.02),
        'dt_bias': dt_bias,
        'a_log': a_log,
        'd_skip': gain((DEPTH, SSM_HEADS)),
        'norm_b_w': gain((DEPTH, D_B)),
        'w_b_out': nrm((DEPTH, D_B, D_MODEL), D_B ** -0.5),
        'q_a_norm_w': gain((DEPTH, Q_LORA)),
        'w_uq': nrm((DEPTH, Q_LORA, MLA_HEADS * QK_DIM), Q_LORA ** -0.5),
        'kv_a_norm_w': gain((DEPTH, KV_LORA)),
        'w_uk': nrm((DEPTH, KV_LORA, MLA_HEADS, NOPE_DIM), KV_LORA ** -0.5),
        'w_uv': nrm((DEPTH, KV_LORA, MLA_HEADS, V_DIM), KV_LORA ** -0.5),
        'q_norm_w': gain((DEPTH, QK_DIM)),
        'k_norm_w': gain((DEPTH, QK_DIM)),
        'w_c_out': nrm((DEPTH, D_C, D_MODEL), D_C ** -0.5),
        'w_out': nrm((DEPTH, D_MODEL, D_MODEL), D_MODEL ** -0.5),
    }


def reference(x_prompt, x_sample, state_conv_a, state_conv_b, state_ssm, cache_latent, cache_krope, page_table,
              norm_w, w_in, conv_a_w, conv_a_b, ln_a_w, ln_a_b, w_a_out, conv_b_w, conv_b_b, dt_bias, a_log,
              d_skip, norm_b_w, w_b_out, q_a_norm_w, w_uq, kv_a_norm_w, w_uk, w_uv, q_norm_w, k_norm_w,
              w_c_out, w_out):
    params = {'norm_w': norm_w, 'w_in': w_in, 'conv_a_w': conv_a_w, 'conv_a_b': conv_a_b, 'ln_a_w': ln_a_w,
              'ln_a_b': ln_a_b, 'w_a_out': w_a_out, 'conv_b_w': conv_b_w, 'conv_b_b': conv_b_b,
              'dt_bias': dt_bias, 'a_log': a_log, 'd_skip': d_skip, 'norm_b_w': norm_b_w, 'w_b_out': w_b_out,
              'q_a_norm_w': q_a_norm_w, 'w_uq': w_uq, 'kv_a_norm_w': kv_a_norm_w, 'w_uk': w_uk, 'w_uv': w_uv,
              'q_norm_w': q_norm_w, 'k_norm_w': k_norm_w, 'w_c_out': w_c_out, 'w_out': w_out}
    b_p, t_p, _ = x_prompt.shape
    t_s = x_sample.shape[1]
    past = page_table.shape[1] * cache_latent.shape[2]
    pos_p = jnp.arange(t_p, dtype=jnp.int32)
    pos_sq = past + jnp.arange(t_s, dtype=jnp.int32)
    pos_sk = jnp.arange(past + t_s, dtype=jnp.int32)
    zero_a = jnp.zeros((b_p, CONV_A_WIDTH - 1, D_A), x_prompt.dtype)
    zero_b = jnp.zeros((b_p, CONV_B_WIDTH - 1, CONV_B_DIM), x_prompt.dtype)
    zero_h = jnp.zeros((b_p, SSM_HEADS, SSM_HEAD_DIM, SSM_STATE), x_prompt.dtype)

    def attend_prompt(q, lat, kr, p):
        return lax.map(lambda a: mla_sequence(a[0], a[1], a[2], pos_p, pos_p, p), (q, lat, kr))

    def make_attend_sample(cache_lat_l, cache_kr_l):
        def attend(q, lat, kr, p):
            def one(a):
                qq, ln, kn, pages = a
                lat_all = jnp.concatenate([cache_lat_l[pages].reshape(-1, KV_LORA).astype(ln.dtype), ln], axis=0)
                kr_all = jnp.concatenate([cache_kr_l[pages].reshape(-1, ROPE_DIM).astype(kn.dtype), kn], axis=0)
                return mla_sequence(qq, lat_all, kr_all, pos_sq, pos_sk, p)
            return lax.map(one, (q, lat, kr, page_table))
        return attend

    y_p = x_prompt
    y_s = x_sample
    outs_p = ([], [], [], [], [])
    outs_s = ([], [], [], [], [])
    for l in range(DEPTH):
        p = {name: arr[l] for name, arr in params.items()}
        y_p, *st_p = mixer_layer(y_p, zero_a, zero_b, zero_h, attend_prompt, p)
        for lst, v in zip(outs_p, st_p):
            lst.append(v)
        y_s, *st_s = mixer_layer(y_s, state_conv_a[l], state_conv_b[l], state_ssm[l],
                                 make_attend_sample(cache_latent[l], cache_krope[l]), p)
        for lst, v in zip(outs_s, st_s):
            lst.append(v)
    prompt_conv_a, prompt_conv_b, prompt_ssm, prompt_latent, prompt_krope = [jnp.stack(v, axis=0) for v in outs_p]
    sample_conv_a, sample_conv_b, sample_ssm, sample_latent, sample_krope = [jnp.stack(v, axis=0) for v in outs_s]
    return (y_p, y_s, prompt_conv_a, prompt_conv_b, prompt_ssm, prompt_latent, prompt_krope,
            sample_conv_a, sample_conv_b, sample_ssm, sample_latent, sample_krope)
```

```python
import functools

import jax
import jax.numpy as jnp
from jax import lax
from jax.experimental import pallas as pl
from jax.experimental.pallas import tpu as pltpu

F32 = jnp.float32
BF16 = jnp.bfloat16
HIGHEST = lax.Precision.HIGHEST

D_MODEL = 1024
D_A = 512
CONV_A_WIDTH = 31
D_B = 1024
SSM_HEAD_DIM = 64
SSM_HEADS = 16
SSM_GROUPS = 2
SSM_STATE = 128
CONV_B_WIDTH = 4
CONV_B_DIM = D_B + 2 * SSM_GROUPS * SSM_STATE
MLA_HEADS = 8
NOPE_DIM = 64
ROPE_DIM = 32
HALF_ROPE = ROPE_DIM // 2
QK_DIM = NOPE_DIM + ROPE_DIM
V_DIM = 64
D_C = MLA_HEADS * V_DIM
Q_LORA = 384
KV_LORA = 256
ROPE_BASE = 10000.0
EPS = 1e-6
N_BRANCH = 3
PAGE = 128

LANES = 128
HEAD_BLOCK = LANES
QK_PAD = MLA_HEADS * HEAD_BLOCK
W1_COLS = 3 * D_A + D_B + CONV_B_DIM
OFF_CQ = 0
OFF_CKV = OFF_CQ + Q_LORA
OFF_CS = OFF_CKV + KV_LORA
OFF_MG = OFF_CS + D_C
OFF_KR = OFF_MG + N_BRANCH * D_MODEL
OFF_DT = OFF_KR + LANES
W2_COLS = OFF_DT + LANES

VMEM_LIMIT = 56 * 1024 * 1024


def _dot(a, b):
    return jnp.dot(a.astype(BF16), b.astype(BF16), preferred_element_type=F32)


def _dot_nt(a, b):
    return lax.dot_general(a.astype(BF16), b.astype(BF16), (((1,), (1,)), ((), ())),
                           preferred_element_type=F32)


def _sigmoid(x):
    return 1.0 / (1.0 + jnp.exp(-x))


def _silu(x):
    return x * _sigmoid(x)


def _rms(x, w):
    return x * lax.rsqrt(jnp.mean(x * x, axis=-1, keepdims=True) + EPS) * w


def _expand_heads(x, rows):
    lane = lax.broadcasted_iota(jnp.int32, (rows, LANES), 1)
    blocks = []
    for i in range(SSM_HEADS // 2):
        blocks.append(jnp.where(lane < SSM_HEAD_DIM, x[:, 2 * i:2 * i + 1], x[:, 2 * i + 1:2 * i + 2]))
    return jnp.concatenate(blocks, axis=1)


def _const_spec(shape, layer=None):
    if layer is None:
        return pl.BlockSpec(shape, lambda *_: (0,) * len(shape), pipeline_mode=pl.Buffered(1))
    return pl.BlockSpec((None,) + shape, lambda *_: (layer,) + (0,) * len(shape),
                        pipeline_mode=pl.Buffered(1))


def _params(n_axes):
    return pltpu.CompilerParams(dimension_semantics=("arbitrary",) * n_axes,
                                vmem_limit_bytes=VMEM_LIMIT)


def _norm_rope_block(val, w_row, cosb, sinb, lane):
    ss = jnp.sum(val * val, axis=-1, keepdims=True)
    vn = val * lax.rsqrt(ss * (1.0 / QK_DIM) + EPS) * w_row
    sw = jnp.where(lane < NOPE_DIM + HALF_ROPE,
                   pltpu.roll(vn, LANES - HALF_ROPE, 1), pltpu.roll(vn, HALF_ROPE, 1))
    return vn * cosb + sw * sinb


def _in_kernel(with_abs, x_ref, nw_ref, w1_ref, w2_ref, dtb_ref, qaw_ref, wuq_ref, kvw_ref, wuk_ref,
               wuv_ref, qnw_ref, knw_ref, cos_ref, sin_ref, *rest):
    if with_abs:
        knw_nope_ref, wabs_ref = rest[:2]
        rest = rest[2:]
    (glu_ref, sa_ref, sz_ref, xbc_ref, dtv_ref, q_ref, k_ref, v_ref, lat_ref, kr_ref, sc_ref,
     g_ref) = rest[:12]
    tm = x_ref.shape[0]
    x = x_ref[...]
    hb = _rms(x, nw_ref[...]).astype(BF16)

    a_val = _dot(hb, w1_ref[:, 0:D_A])
    a_gate = _dot(hb, w1_ref[:, D_A:2 * D_A])
    glu_ref[...] = a_val * _sigmoid(a_gate)
    sa_ref[...] = _silu(_dot(hb, w1_ref[:, 2 * D_A:3 * D_A]))
    sz_ref[...] = _silu(_dot(hb, w1_ref[:, 3 * D_A:3 * D_A + D_B]))
    xbc_ref[...] = _dot(hb, w1_ref[:, 3 * D_A + D_B:W1_COLS])

    cq = _dot(hb, w2_ref[:, OFF_CQ:OFF_CKV])
    ckv = _dot(hb, w2_ref[:, OFF_CKV:OFF_CS])
    sc_ref[...] = _silu(_dot(hb, w2_ref[:, OFF_CS:OFF_MG]))
    g_ref[...] = _sigmoid(_dot(hb, w2_ref[:, OFF_MG:OFF_KR]))
    krp = _dot(hb, w2_ref[:, OFF_KR:OFF_DT])
    dtp = _dot(hb, w2_ref[:, OFF_DT:W2_COLS])
    kr_ref[...] = krp[:, NOPE_DIM:QK_DIM]
    dtx = dtp + dtb_ref[...]
    dtv_ref[...] = jnp.maximum(dtx, 0.0) + jnp.log1p(jnp.exp(-jnp.abs(dtx)))

    q = _dot(_rms(cq, qaw_ref[...]), wuq_ref[...])
    latn = _rms(ckv, kvw_ref[...])
    lat_ref[...] = latn
    latb = latn.astype(BF16)
    kn = _dot(latb, wuk_ref[...])
    v_ref[...] = _dot(latb, wuv_ref[...]).astype(BF16)

    cosb = cos_ref[...]
    sinb = sin_ref[...]
    lane = lax.broadcasted_iota(jnp.int32, (tm, LANES), 1)
    scale = QK_DIM ** -0.5
    for h in range(MLA_HEADS):
        sl = slice(h * HEAD_BLOCK, (h + 1) * HEAD_BLOCK)
        qr = _norm_rope_block(q[:, sl], qnw_ref[...], cosb, sinb, lane) * scale
        q_ref[:, sl] = qr.astype(BF16)
        kr = _norm_rope_block(kn[:, sl] + krp, knw_ref[...], cosb, sinb, lane)
        k_ref[:, sl] = kr.astype(BF16)
        if with_abs:
            qabs_ref = rest[12]
            qa = _dot(qr * knw_nope_ref[...], wabs_ref[sl, :])
            qabs_ref[:, h * KV_LORA:(h + 1) * KV_LORA] = qa.astype(BF16)


def _in_proj(x, wp, layer, cos_tab, sin_tab, with_abs, tm=256):
    n = x.shape[0]
    assert n % tm == 0 and cos_tab.shape[0] % tm == 0
    nper = cos_tab.shape[0] // tm
    row = lambda w: pl.BlockSpec((tm, w), lambda i: (i, 0))
    tab = pl.BlockSpec((tm, LANES), lambda i: (i % nper, 0))
    in_specs = [
        row(D_MODEL),
        _const_spec((1, D_MODEL), layer),
        _const_spec((D_MODEL, W1_COLS), layer),
        _const_spec((D_MODEL, W2_COLS), layer),
        _const_spec((1, LANES), layer),
        _const_spec((1, Q_LORA), layer),
        _const_spec((Q_LORA, QK_PAD), layer),
        _const_spec((1, KV_LORA), layer),
        _const_spec((KV_LORA, QK_PAD), layer),
        _const_spec((KV_LORA, D_C), layer),
        _const_spec((1, LANES), layer),
        _const_spec((1, LANES), layer),
        tab, tab,
    ]
    args = [x, wp['norm_w'], wp['w1'], wp['w2'], wp['dt_bias'], wp['q_a_norm_w'], wp['w_uq'],
            wp['kv_a_norm_w'], wp['w_uk'], wp['w_uv'], wp['q_norm_w'], wp['k_norm_w'], cos_tab, sin_tab]
    outs = [(D_A, F32), (D_A, F32), (D_B, F32), (CONV_B_DIM, F32), (LANES, F32), (QK_PAD, BF16),
            (QK_PAD, BF16), (D_C, BF16), (KV_LORA, F32), (ROPE_DIM, F32), (D_C, F32),
            (N_BRANCH * D_MODEL, F32)]
    if with_abs:
        in_specs += [_const_spec((1, LANES), layer), _const_spec((QK_PAD, KV_LORA), layer)]
        args += [wp['k_norm_w_nope'], wp['w_uk_abs']]
        outs.append((MLA_HEADS * KV_LORA, BF16))
    return pl.pallas_call(
        functools.partial(_in_kernel, with_abs),
        out_shape=[jax.ShapeDtypeStruct((n, w), dt) for w, dt in outs],
        grid=(n // tm,),
        in_specs=in_specs,
        out_specs=[row(w) for w, _ in outs],
        compiler_params=_params(1),
    )(*args)


HIST_A = CONV_A_WIDTH - 1
HIST_A_PAD = 32


def _conv_a_kernel(glu_ref, sa_ref, hist_ref, cw_ref, cb_ref, lw_ref, lb_ref, ya_ref, newa_ref, buf):
    t = pl.program_id(1)
    tt = glu_ref.shape[0]
    lo = HIST_A_PAD - HIST_A

    @pl.when(t == 0)
    def _():
        buf[lo:HIST_A_PAD, :] = hist_ref[0]

    buf[HIST_A_PAD:HIST_A_PAD + tt, :] = glu_ref[...]
    acc = jnp.broadcast_to(cb_ref[...], (tt, D_A))
    for k in range(CONV_A_WIDTH):
        acc = acc + cw_ref[k:k + 1, :] * buf[lo + k:lo + k + tt, :]
    xc = acc - jnp.mean(acc, axis=-1, keepdims=True)
    ln = xc * lax.rsqrt(jnp.mean(xc * xc, axis=-1, keepdims=True) + EPS) * lw_ref[...] + lb_ref[...]
    ya_ref[...] = _silu(ln) * sa_ref[...]
    tail = buf[tt + lo:tt + HIST_A_PAD, :]

    @pl.when(t == pl.num_programs(1) - 1)
    def _():
        newa_ref[0] = tail

    buf[lo:HIST_A_PAD, :] = tail


def _conv_a(glu, sa, hist, wp, layer, batch, tt):
    n = glu.shape[0]
    nt = n // (batch * tt)
    row = pl.BlockSpec((tt, D_A), lambda b, t: (b * nt + t, 0))
    return pl.pallas_call(
        _conv_a_kernel,
        out_shape=[jax.ShapeDtypeStruct((n, D_A), F32),
                   jax.ShapeDtypeStruct((batch, HIST_A, D_A), F32)],
        grid=(batch, nt),
        in_specs=[row, row,
                  pl.BlockSpec((1, HIST_A, D_A), lambda b, t: (b, 0, 0)),
                  _const_spec((CONV_A_WIDTH, D_A), layer),
                  _const_spec((1, D_A), layer), _const_spec((1, D_A), layer),
                  _const_spec((1, D_A), layer)],
        out_specs=[row, pl.BlockSpec((1, HIST_A, D_A), lambda b, t: (b, 0, 0))],
        scratch_shapes=[pltpu.VMEM((HIST_A_PAD + tt, D_A), F32)],
        compiler_params=_params(2),
    )(glu, sa, hist, wp['conv_a_w'], wp['conv_a_b'], wp['ln_a_w'], wp['ln_a_b'])


HIST_B = CONV_B_WIDTH - 1
HIST_B_PAD = 8
GROUP_COLS = D_B // SSM_GROUPS
HEADS_PER_GROUP = SSM_HEADS // SSM_GROUPS


def _conv_b(cbuf, xbc_ref, hist_loader, cw_ref, cb_ref, q, first):
    lo = HIST_B_PAD - HIST_B

    if first is None:
        cbuf[lo:HIST_B_PAD, :] = hist_loader()
    else:
        @pl.when(first)
        def _():
            cbuf[lo:HIST_B_PAD, :] = hist_loader()

    cbuf[HIST_B_PAD:HIST_B_PAD + q, :] = xbc_ref[...]
    acc = jnp.broadcast_to(cb_ref[...], (q, CONV_B_DIM))
    for k in range(CONV_B_WIDTH):
        acc = acc + cw_ref[k:k + 1, :] * cbuf[lo + k:lo + k + q, :]
    tail = cbuf[q + lo:q + HIST_B_PAD, :]
    return _silu(acc), tail


def _ssd_p_kernel(xbc_ref, hist_ref, cw_ref, cb_ref, dtv_ref, a_ref, dsk_ref, sz_ref, nbw_ref, h0_ref,
                  yb_ref, newb_ref, hfin_ref, cbuf, ht):
    c = pl.program_id(1)
    last = c == pl.num_programs(1) - 1
    q = xbc_ref.shape[0]
    xbc, tail = _conv_b(cbuf, xbc_ref, lambda: hist_ref[0], cw_ref, cb_ref, q, c == 0)

    @pl.when(c == 0)
    def _():
        ht[...] = h0_ref[0].reshape(SSM_HEADS * SSM_HEAD_DIM, SSM_STATE).T

    @pl.when(last)
    def _():
        newb_ref[0] = tail

    cbuf[HIST_B_PAD - HIST_B:HIST_B_PAD, :] = tail

    xs = xbc[:, :D_B]
    bs = xbc[:, D_B:D_B + SSM_GROUPS * SSM_STATE]
    cs = xbc[:, D_B + SSM_GROUPS * SSM_STATE:]
    dt = dtv_ref[...]
    adt = dt * a_ref[...]
    row = lax.broadcasted_iota(jnp.int32, (q, q), 0)
    col = lax.broadcasted_iota(jnp.int32, (q, q), 1)
    tril = row >= col
    acum = jnp.dot(tril.astype(F32), adt, precision=HIGHEST, preferred_element_type=F32)
    acum_t = acum.T
    dt_t = dt.T
    alast = acum[q - 1:q, :]
    dec_end = jnp.exp(alast - acum) * dt
    xsb = xs.astype(BF16)
    xw = (xs * _expand_heads(dec_end, q)).astype(BF16)
    chunk_decay = _expand_heads(jnp.exp(alast), 1)
    lane4 = lax.broadcasted_iota(jnp.int32, (q, 4 * SSM_HEAD_DIM), 1) // SSM_HEAD_DIM

    y_diag = []
    y_off = []
    for g in range(SSM_GROUPS):
        bsg = bs[:, g * SSM_STATE:(g + 1) * SSM_STATE]
        csg = cs[:, g * SSM_STATE:(g + 1) * SSM_STATE]
        cb = _dot_nt(csg, bsg)
        for quad in range(HEADS_PER_GROUP // 4):
            ws = []
            rhs = []
            qd = g * (HEADS_PER_GROUP // 4) + quad
            xblk = xsb[:, qd * 4 * SSM_HEAD_DIM:(qd + 1) * 4 * SSM_HEAD_DIM]
            for hh in range(4):
                h = qd * 4 + hh
                seg = acum[:, h:h + 1] - acum_t[h:h + 1, :]
                lm = jnp.exp(jnp.where(tril, seg, -jnp.inf))
                ws.append((cb * lm * dt_t[h:h + 1, :]).astype(BF16))
                rhs.append(jnp.where(lane4 == hh, xblk, jnp.zeros_like(xblk)))
            y_diag.append(jnp.dot(jnp.concatenate(ws, axis=1), jnp.concatenate(rhs, axis=0),
                                  preferred_element_type=F32))
        gs = slice(g * GROUP_COLS, (g + 1) * GROUP_COLS)
        htg = ht[:, gs]
        y_off.append(_dot(csg, htg))
        ht[:, gs] = htg * chunk_decay[:, gs] + _dot(bsg.T, xw[:, gs])

    y = (jnp.concatenate(y_diag, axis=1)
         + jnp.concatenate(y_off, axis=1) * _expand_heads(jnp.exp(acum), q)
         + dsk_ref[...] * xs)
    yb_ref[...] = _rms(y * sz_ref[...], nbw_ref[...])

    @pl.when(last)
    def _():
        hfin_ref[0] = ht[...].T.reshape(SSM_HEADS, SSM_HEAD_DIM, SSM_STATE)


def _ssd_prompt(xbc, dtv, sz, hist, h0, wp, layer, batch, q=128):
    n = xbc.shape[0]
    nc = n // (batch * q)
    row = lambda w: pl.BlockSpec((q, w), lambda b, c: (b * nc + c, 0))
    state = pl.BlockSpec((1, SSM_HEADS, SSM_HEAD_DIM, SSM_STATE), lambda b, c: (b, 0, 0, 0))
    histb = pl.BlockSpec((1, HIST_B, CONV_B_DIM), lambda b, c: (b, 0, 0))
    return pl.pallas_call(
        _ssd_p_kernel,
        out_shape=[jax.ShapeDtypeStruct((n, D_B), F32),
                   jax.ShapeDtypeStruct((batch, HIST_B, CONV_B_DIM), F32),
                   jax.ShapeDtypeStruct((batch, SSM_HEADS, SSM_HEAD_DIM, SSM_STATE), F32)],
        grid=(batch, nc),
        in_specs=[row(CONV_B_DIM), histb,
                  _const_spec((CONV_B_WIDTH, CONV_B_DIM), layer), _const_spec((1, CONV_B_DIM), layer),
                  row(LANES), _const_spec((1, LANES), layer), _const_spec((1, D_B), layer),
                  row(D_B), _const_spec((1, D_B), layer), state],
        out_specs=[row(D_B), histb, state],
        scratch_shapes=[pltpu.VMEM((HIST_B_PAD + q, CONV_B_DIM), F32),
                        pltpu.VMEM((SSM_STATE, D_B), F32)],
        compiler_params=_params(2),
    )(xbc, hist, wp['conv_b_w'], wp['conv_b_b'], dtv, wp['a_neg'], wp['d_skip'], sz, wp['norm_b_w'], h0)


def _ssd_s_kernel(xbc_ref, hist_ref, cw_ref, cb_ref, dtv_ref, a_ref, dsk_ref, sz_ref, nbw_ref, h0_ref,
                  yb_ref, newb_ref, hfin_ref, cbuf):
    q = xbc_ref.shape[0]
    assert q * SSM_HEADS == LANES
    xbc, tail = _conv_b(cbuf, xbc_ref, lambda: hist_ref[0], cw_ref, cb_ref, q, None)
    newb_ref[0] = tail
    xs = xbc[:, :D_B]
    bs = xbc[:, D_B:D_B + SSM_GROUPS * SSM_STATE]
    cs = xbc[:, D_B + SSM_GROUPS * SSM_STATE:]
    dt = dtv_ref[...]
    adt = dt * a_ref[...]
    row = lax.broadcasted_iota(jnp.int32, (q, LANES), 0)
    lane = lax.broadcasted_iota(jnp.int32, (q, LANES), 1)
    tril8 = (row >= lane).astype(F32)[:, :q]
    acum = jnp.dot(tril8, adt, precision=HIGHEST, preferred_element_type=F32)
    er = lax.broadcasted_iota(jnp.int32, (LANES, LANES), 0)
    ec = lax.broadcasted_iota(jnp.int32, (LANES, LANES), 1)
    rep = (er == ec // q).astype(F32)
    a_col = jnp.dot(acum, rep, precision=HIGHEST, preferred_element_type=F32)
    d_col = jnp.dot(dt, rep, precision=HIGHEST, preferred_element_type=F32)
    sel = row == lane % q
    a_row = jnp.sum(jnp.where(sel, a_col, 0.0), axis=0, keepdims=True)
    d_row = jnp.sum(jnp.where(sel, d_col, 0.0), axis=0, keepdims=True)
    lm = jnp.exp(jnp.where(row >= lane % q, a_col - a_row, -jnp.inf))
    br = lax.broadcasted_iota(jnp.int32, (LANES, SSM_GROUPS * SSM_STATE), 0)
    bc = lax.broadcasted_iota(jnp.int32, (LANES, SSM_GROUPS * SSM_STATE), 1)
    bs_t = jnp.where(br // (q * HEADS_PER_GROUP) == bc // SSM_STATE,
                     jnp.concatenate([bs] * SSM_HEADS, axis=0), 0.0)
    cb_flat = _dot_nt(cs, bs_t)
    w = (cb_flat * lm * d_row).astype(BF16)
    xr = lax.broadcasted_iota(jnp.int32, (LANES, D_B), 0)
    xc = lax.broadcasted_iota(jnp.int32, (LANES, D_B), 1)
    rhs = jnp.where(xr // q == xc // SSM_HEAD_DIM, jnp.concatenate([xs] * SSM_HEADS, axis=0), 0.0)
    y_diag = jnp.dot(w, rhs.astype(BF16), preferred_element_type=F32)

    h0 = h0_ref[0].reshape(SSM_HEADS * SSM_HEAD_DIM, SSM_STATE)
    alast = acum[q - 1:q, :]
    ea_last = jnp.exp(alast)
    dec_end = jnp.exp(alast - acum) * dt
    xw = (xs * _expand_heads(dec_end, q)).astype(BF16)
    ir = lax.broadcasted_iota(jnp.int32, (GROUP_COLS, GROUP_COLS), 0)
    ic = lax.broadcasted_iota(jnp.int32, (GROUP_COLS, GROUP_COLS), 1)
    eye = (ir == ic).astype(BF16)
    y_off = []
    for g in range(SSM_GROUPS):
        gs = slice(g * GROUP_COLS, (g + 1) * GROUP_COLS)
        bsg = bs[:, g * SSM_STATE:(g + 1) * SSM_STATE]
        csg = cs[:, g * SSM_STATE:(g + 1) * SSM_STATE]
        y_off.append(_dot_nt(csg, h0[gs, :]))
        xw_t = _dot_nt(eye, xw[:, gs])
        st = _dot(xw_t, bsg)
        for hh in range(HEADS_PER_GROUP):
            h = g * HEADS_PER_GROUP + hh
            rs = slice(hh * SSM_HEAD_DIM, (hh + 1) * SSM_HEAD_DIM)
            hfin_ref[0, h] = h0_ref[0, h] * ea_last[:, h:h + 1] + st[rs, :]
    y = (y_diag + jnp.concatenate(y_off, axis=1) * _expand_heads(jnp.exp(acum), q)
         + dsk_ref[...] * xs)
    yb_ref[...] = _rms(y * sz_ref[...], nbw_ref[...])


def _ssd_sample(xbc, dtv, sz, hist, h0, wp, layer, batch, q):
    n = xbc.shape[0]
    row = lambda w: pl.BlockSpec((q, w), lambda b: (b, 0))
    state = pl.BlockSpec((1, SSM_HEADS, SSM_HEAD_DIM, SSM_STATE), lambda b: (b, 0, 0, 0))
    histb = pl.BlockSpec((1, HIST_B, CONV_B_DIM), lambda b: (b, 0, 0))
    return pl.pallas_call(
        _ssd_s_kernel,
        out_shape=[jax.ShapeDtypeStruct((n, D_B), F32),
                   jax.ShapeDtypeStruct((batch, HIST_B, CONV_B_DIM), F32),
                   jax.ShapeDtypeStruct((batch, SSM_HEADS, SSM_HEAD_DIM, SSM_STATE), F32)],
        grid=(batch,),
        in_specs=[row(CONV_B_DIM), histb,
                  _const_spec((CONV_B_WIDTH, CONV_B_DIM), layer), _const_spec((1, CONV_B_DIM), layer),
                  row(LANES), _const_spec((1, LANES), layer), _const_spec((1, D_B), layer),
                  row(D_B), _const_spec((1, D_B), layer), state],
        out_specs=[row(D_B), histb, state],
        scratch_shapes=[pltpu.VMEM((HIST_B_PAD + q, CONV_B_DIM), F32)],
        compiler_params=_params(1),
    )(xbc, hist, wp['conv_b_w'], wp['conv_b_b'], dtv, wp['a_neg'], wp['d_skip'], sz, wp['norm_b_w'], h0)


def _attn_p_kernel(q_ref, k_ref, v_ref, o_ref):
    i = pl.program_id(1)
    tq = q_ref.shape[0]
    tk = tq
    row = lax.broadcasted_iota(jnp.int32, (tq, tk), 0)
    col = lax.broadcasted_iota(jnp.int32, (tq, tk), 1)
    causal = row >= col
    lane = lax.broadcasted_iota(jnp.int32, (tq, LANES), 1)
    for p in range(MLA_HEADS // 2):
        outs = []
        for e in range(2):
            h = 2 * p + e
            hs = slice(h * HEAD_BLOCK, (h + 1) * HEAD_BLOCK)
            ps = slice(p * LANES, (p + 1) * LANES)
            qh = q_ref[:, hs]

            def tile(j, carry, masked, hs=hs, ps=ps, qh=qh):
                m, l, acc = carry
                start = pl.multiple_of(j * tk, tk)
                kh = k_ref[pl.ds(start, tk), hs]
                vh = v_ref[pl.ds(start, tk), ps]
                s = _dot_nt(qh, kh)
                if masked:
                    s = jnp.where(causal, s, -jnp.inf)
                m_new = jnp.maximum(m, jnp.max(s, axis=-1, keepdims=True))
                alpha = jnp.exp(m - m_new)
                pr = jnp.exp(s - m_new)
                l = alpha * l + jnp.sum(pr, axis=-1, keepdims=True)
                acc = alpha * acc + jnp.dot(pr.astype(BF16), vh, preferred_element_type=F32)
                return m_new, l, acc

            init = (jnp.full((tq, 1), -1e30, F32), jnp.zeros((tq, 1), F32), jnp.zeros((tq, LANES), F32))
            carry = lax.fori_loop(0, i, lambda j, c: tile(j, c, False), init)
            _, l, acc = tile(i, carry, True)
            outs.append(acc / l)
        o_ref[:, p * LANES:(p + 1) * LANES] = jnp.where(lane < V_DIM, outs[0], outs[1])


def _attn_prompt(q, k, v, batch, seq, tq=256):
    n = q.shape[0]
    nq = seq // tq
    return pl.pallas_call(
        _attn_p_kernel,
        out_shape=jax.ShapeDtypeStruct((n, D_C), F32),
        grid=(batch, nq),
        in_specs=[pl.BlockSpec((tq, QK_PAD), lambda b, i: (b * nq + i, 0)),
                  pl.BlockSpec((seq, QK_PAD), lambda b, i: (b, 0)),
                  pl.BlockSpec((seq, D_C), lambda b, i: (b, 0))],
        out_specs=pl.BlockSpec((tq, D_C), lambda b, i: (b * nq + i, 0)),
        compiler_params=_params(2),
    )(q, k, v)


PAGES_PER_CHUNK = 8
CHUNK = PAGES_PER_CHUNK * PAGE
Q_ROWS = 64
LHS_ROWS = MLA_HEADS * NOPE_DIM + Q_ROWS


def _attn_s_kernel(layer, n_chunks, past, pt_ref, qn_ref, qc_ref, qs_ref, wukt_ref, knwr_ref, cc_ref,
                   ss_ref, latn_ref, krn_ref, clat_hbm, ckr_hbm, ctx_ref,
                   latbuf, krbuf, sems, lhs, newlat, newkr):
    b = pl.program_id(0)
    nb = pl.num_programs(0)
    t_new = latn_ref.shape[1]

    def copies(bb, cc, slot):
        out = []
        for k in range(PAGES_PER_CHUNK):
            page = pt_ref[bb, cc * PAGES_PER_CHUNK + k]
            rows = pl.ds(k * PAGE, PAGE)
            out.append(pltpu.make_async_copy(clat_hbm.at[layer, page], latbuf.at[slot, rows],
                                             sems.at[0, slot]))
            out.append(pltpu.make_async_copy(ckr_hbm.at[layer, page], krbuf.at[slot, rows],
                                             sems.at[1, slot]))
        return out

    @pl.when(b == 0)
    def _():
        for cp in copies(0, 0, 0):
            cp.start()
        lhs[0:MLA_HEADS * NOPE_DIM, :] = wukt_ref[...]
        newlat[...] = jnp.zeros_like(newlat)
        newkr[...] = jnp.zeros_like(newkr)

    lhs[MLA_HEADS * NOPE_DIM:LHS_ROWS, :] = qn_ref[0]
    qc = qc_ref[0]
    qs = qs_ref[0]
    ones = jnp.ones((MLA_HEADS, ROPE_DIM), BF16)
    knwr = knwr_ref[...]

    def process(lat, kr, pos0, carry, mask):
        m, l, acc = carry
        ck = lat.shape[0]
        latb = lat.astype(BF16)
        big = _dot_nt(lhs[...], latb)
        knt = big[0:MLA_HEADS * NOPE_DIM, :]
        ssq = jnp.sum((knt * knt).reshape(MLA_HEADS, NOPE_DIM, ck), axis=1)
        kr2 = kr * kr
        hi = kr2.astype(BF16)
        lo = (kr2 - hi.astype(F32)).astype(BF16)
        ssr = _dot_nt(ones, hi) + _dot_nt(ones, lo)
        r = lax.rsqrt((ssq + ssr) * (1.0 / QK_DIM) + EPS)
        x = kr * knwr
        fc = x * cc_ref[pl.ds(pos0, ck), :]
        fs = x * ss_ref[pl.ds(pos0, ck), :]
        s = big[MLA_HEADS * NOPE_DIM:LHS_ROWS, :] + _dot_nt(qc, fc) + _dot_nt(qs, fs)
        s = s * jnp.concatenate([r] * (Q_ROWS // MLA_HEADS), axis=0)
        if mask is not None:
            s = jnp.where(mask, s, -1e30)
        m_new = jnp.maximum(m, jnp.max(s, axis=-1, keepdims=True))
        alpha = jnp.exp(m - m_new)
        pr = jnp.exp(s - m_new)
        if mask is not None:
            pr = jnp.where(mask, pr, 0.0)
        l = alpha * l + jnp.sum(pr, axis=-1, keepdims=True)
        acc = alpha * acc + jnp.dot(pr.astype(BF16), latb, preferred_element_type=F32)
        return m_new, l, acc

    def body(c, carry):
        g = b * n_chunks + c
        slot = g % 2
        wrap = c == n_chunks - 1
        nxt_b = jnp.where(wrap, b + 1, b)
        nxt_c = jnp.where(wrap, 0, c + 1)

        @pl.when(g + 1 < nb * n_chunks)
        def _():
            for cp in copies(nxt_b, nxt_c, 1 - slot):
                cp.start()

        for cp in copies(b, c, slot):
            cp.wait()
        return process(latbuf[slot], krbuf[slot], pl.multiple_of(c * CHUNK, CHUNK), carry, None)

    init = (jnp.full((Q_ROWS, 1), -1e30, F32), jnp.zeros((Q_ROWS, 1), F32),
            jnp.zeros((Q_ROWS, KV_LORA), F32))
    carry = lax.fori_loop(0, n_chunks, body, init)

    newlat[0:t_new, :] = latn_ref[0]
    newkr[0:t_new, :] = krn_ref[0]
    qi = lax.broadcasted_iota(jnp.int32, (Q_ROWS, PAGE), 0) // MLA_HEADS
    kj = lax.broadcasted_iota(jnp.int32, (Q_ROWS, PAGE), 1)
    _, l, acc = process(newlat[...], newkr[...], past, carry, kj <= qi)
    ctx_ref[0] = acc / l


def _attn_sample(page_table, qn, qc, qs, latn, krn, cache_latent, cache_krope, cc_tab, ss_tab, wp, layer):
    batch, n_pages = page_table.shape
    assert n_pages % PAGES_PER_CHUNK == 0
    n_chunks = n_pages // PAGES_PER_CHUNK
    past = n_pages * PAGE
    t_new = latn.shape[1]
    per_seq = lambda shape: pl.BlockSpec((1,) + shape, lambda b, pt: (b, 0, 0))
    grid_spec = pltpu.PrefetchScalarGridSpec(
        num_scalar_prefetch=1,
        grid=(batch,),
        in_specs=[per_seq((Q_ROWS, KV_LORA)), per_seq((Q_ROWS, ROPE_DIM)), per_seq((Q_ROWS, ROPE_DIM)),
                  _const_spec((MLA_HEADS * NOPE_DIM, KV_LORA), layer),
                  _const_spec((1, ROPE_DIM), layer),
                  _const_spec(cc_tab.shape), _const_spec(ss_tab.shape),
                  per_seq((t_new, KV_LORA)), per_seq((t_new, ROPE_DIM)),
                  pl.BlockSpec(memory_space=pl.ANY), pl.BlockSpec(memory_space=pl.ANY)],
        out_specs=per_seq((Q_ROWS, KV_LORA)),
        scratch_shapes=[pltpu.VMEM((2, CHUNK, KV_LORA), F32),
                        pltpu.VMEM((2, CHUNK, ROPE_DIM), F32),
                        pltpu.SemaphoreType.DMA((2, 2)),
                        pltpu.VMEM((LHS_ROWS, KV_LORA), BF16),
                        pltpu.VMEM((PAGE, KV_LORA), F32),
                        pltpu.VMEM((PAGE, ROPE_DIM), F32)],
    )
    return pl.pallas_call(
        functools.partial(_attn_s_kernel, layer, n_chunks, past),
        out_shape=jax.ShapeDtypeStruct((batch, Q_ROWS, KV_LORA), F32),
        grid_spec=grid_spec,
        compiler_params=_params(1),
    )(page_table, qn, qc, qs, wp['w_uk_t'], wp['k_norm_w_rope'], cc_tab, ss_tab, latn, krn,
      cache_latent, cache_krope)


def _uv_kernel(ctx_ref, wv_ref, att_ref):
    for p in range(MLA_HEADS // 2):
        acc = None
        for e in range(2):
            h = 2 * p + e
            part = _dot(ctx_ref[:, h * KV_LORA:(h + 1) * KV_LORA], wv_ref[h])
            acc = part if acc is None else acc + part
        att_ref[:, p * LANES:(p + 1) * LANES] = acc


def _uv_proj(ctx, wp, layer, tm=256):
    n = ctx.shape[0]
    tm = min(tm, n)
    return pl.pallas_call(
        _uv_kernel,
        out_shape=jax.ShapeDtypeStruct((n, D_C), F32),
        grid=(n // tm,),
        in_specs=[pl.BlockSpec((tm, MLA_HEADS * KV_LORA), lambda i: (i, 0)),
                  _const_spec((MLA_HEADS, KV_LORA, LANES), layer)],
        out_specs=pl.BlockSpec((tm, D_C), lambda i: (i, 0)),
        compiler_params=_params(1),
    )(ctx, wp['w_uv_pair'])


def _out_kernel(x_ref, ya_ref, yb_ref, att_ref, sc_ref, g_ref, wa_ref, wb_ref, wc_ref, wo_ref, y_ref):
    br_a = _dot(ya_ref[...], wa_ref[...])
    br_b = _dot(yb_ref[...], wb_ref[...])
    br_c = _dot(att_ref[...] * sc_ref[...], wc_ref[...])
    merged = (g_ref[:, 0:D_MODEL] * br_a + g_ref[:, D_MODEL:2 * D_MODEL] * br_b
              + g_ref[:, 2 * D_MODEL:3 * D_MODEL] * br_c)
    y_ref[...] = x_ref[...] + _dot(merged, wo_ref[...])


def _out_proj(x, ya, yb, att, sc, g, wp, layer, tm=256):
    n = x.shape[0]
    row = lambda w: pl.BlockSpec((tm, w), lambda i: (i, 0))
    return pl.pallas_call(
        _out_kernel,
        out_shape=jax.ShapeDtypeStruct((n, D_MODEL), F32),
        grid=(n // tm,),
        in_specs=[row(D_MODEL), row(D_A), row(D_B), row(D_C), row(D_C), row(N_BRANCH * D_MODEL),
                  _const_spec((D_A, D_MODEL), layer), _const_spec((D_B, D_MODEL), layer),
                  _const_spec((D_C, D_MODEL), layer), _const_spec((D_MODEL, D_MODEL), layer)],
        out_specs=row(D_MODEL),
        compiler_params=_params(1),
    )(x, ya, yb, att, sc, g, wp['w_a_out'], wp['w_b_out'], wp['w_c_out'], wp['w_out'])


def _pad_lanes(row, width=LANES):
    return jnp.pad(row, ((0, 0), (0, 0), (0, width - row.shape[-1])))


def _prepare_weights(w_in, dt_bias, a_log, d_skip, w_uq, w_uk, w_uv, q_norm_w, k_norm_w, others):
    depth = w_in.shape[0]
    o = 0
    cuts = {}
    for name, width in (('a', 3 * D_A), ('z', D_B), ('xbc', CONV_B_DIM), ('dt', SSM_HEADS), ('cq', Q_LORA),
                        ('ckv', KV_LORA), ('kr', ROPE_DIM), ('cs', D_C), ('mg', N_BRANCH * D_MODEL)):
        cuts[name] = (o, o + width)
        o += width
    col = lambda name: w_in[:, :, cuts[name][0]:cuts[name][1]]
    zeros = lambda width: jnp.zeros((depth, D_MODEL, width), w_in.dtype)
    w1 = w_in[:, :, 0:W1_COLS].astype(BF16)
    w2 = jnp.concatenate([col('cq'), col('ckv'), col('cs'), col('mg'),
                          zeros(NOPE_DIM), col('kr'), zeros(LANES - QK_DIM),
                          col('dt'), zeros(LANES - SSM_HEADS)], axis=-1).astype(BF16)
    head_pad = lambda w: jnp.pad(w, ((0, 0), (0, 0), (0, 0), (0, HEAD_BLOCK - w.shape[-1])))
    wuq = head_pad(w_uq.reshape(depth, Q_LORA, MLA_HEADS, QK_DIM)).reshape(depth, Q_LORA, QK_PAD)
    wuk = head_pad(w_uk).reshape(depth, KV_LORA, QK_PAD)
    wuk_abs = jnp.transpose(head_pad(w_uk), (0, 2, 3, 1)).reshape(depth, QK_PAD, KV_LORA)
    wuk_t = jnp.transpose(w_uk, (0, 2, 3, 1)).reshape(depth, MLA_HEADS * NOPE_DIM, KV_LORA)
    wuv = w_uv.reshape(depth, KV_LORA, D_C)
    wv_h = jnp.transpose(w_uv, (0, 2, 1, 3))
    left = jnp.pad(wv_h, ((0, 0), (0, 0), (0, 0), (0, V_DIM)))
    right = jnp.pad(wv_h, ((0, 0), (0, 0), (0, 0), (V_DIM, 0)))
    even = (jnp.arange(MLA_HEADS) % 2 == 0)[None, :, None, None]
    wuv_pair = jnp.where(even, left, right)
    wp = dict(others)
    wp.update(
        w1=w1, w2=w2,
        dt_bias=_pad_lanes(dt_bias[:, None, :]),
        a_neg=_pad_lanes(-jnp.exp(a_log.astype(F32))[:, None, :]),
        d_skip=jnp.repeat(d_skip, SSM_HEAD_DIM, axis=-1)[:, None, :],
        w_uq=wuq.astype(BF16), w_uk=wuk.astype(BF16), w_uv=wuv.astype(BF16),
        w_uk_abs=wuk_abs.astype(BF16), w_uk_t=wuk_t.astype(BF16), w_uv_pair=wuv_pair.astype(BF16),
        q_norm_w=_pad_lanes(q_norm_w[:, None, :]), k_norm_w=_pad_lanes(k_norm_w[:, None, :]),
        k_norm_w_nope=_pad_lanes(k_norm_w[:, None, :NOPE_DIM]),
        k_norm_w_rope=k_norm_w[:, None, NOPE_DIM:],
    )
    return wp


def _rope_angles(pos):
    inv = jnp.power(ROPE_BASE, -jnp.arange(HALF_ROPE, dtype=F32) / HALF_ROPE)
    ang = pos.astype(F32)[:, None] * inv[None, :]
    return jnp.cos(ang), jnp.sin(ang)


def _head_block_tables(pos):
    cos, sin = _rope_angles(pos)
    n = pos.shape[0]
    cos_tab = jnp.concatenate([jnp.ones((n, NOPE_DIM), F32), cos, cos,
                               jnp.ones((n, LANES - QK_DIM), F32)], axis=1)
    sin_tab = jnp.concatenate([jnp.zeros((n, NOPE_DIM), F32), -sin, sin,
                               jnp.zeros((n, LANES - QK_DIM), F32)], axis=1)
    return cos_tab, sin_tab


def kernel(x_prompt, x_sample, state_conv_a, state_conv_b, state_ssm, cache_latent, cache_krope, page_table,
           norm_w, w_in, conv_a_w, conv_a_b, ln_a_w, ln_a_b, w_a_out, conv_b_w, conv_b_b, dt_bias, a_log,
           d_skip, norm_b_w, w_b_out, q_a_norm_w, w_uq, kv_a_norm_w, w_uk, w_uv, q_norm_w, k_norm_w,
           w_c_out, w_out):
    depth = w_in.shape[0]
    b_p, t_p, _ = x_prompt.shape
    b_s, t_s, _ = x_sample.shape
    n_pages = page_table.shape[1]
    past = n_pages * PAGE
    tm = 256
    assert (b_s * t_s) % tm == 0 and tm % t_s == 0 and t_p % tm == 0

    row3 = lambda w: w[:, None, :]
    others = dict(
        norm_w=row3(norm_w), conv_a_w=conv_a_w, conv_a_b=row3(conv_a_b), ln_a_w=row3(ln_a_w),
        ln_a_b=row3(ln_a_b), conv_b_w=conv_b_w, conv_b_b=row3(conv_b_b), norm_b_w=row3(norm_b_w),
        q_a_norm_w=row3(q_a_norm_w), kv_a_norm_w=row3(kv_a_norm_w),
        w_a_out=w_a_out.astype(BF16), w_b_out=w_b_out.astype(BF16), w_c_out=w_c_out.astype(BF16),
        w_out=w_out.astype(BF16))
    wp = _prepare_weights(w_in, dt_bias, a_log, d_skip, w_uq, w_uk, w_uv, q_norm_w, k_norm_w, others)

    cos_p, sin_p = _head_block_tables(jnp.arange(t_p, dtype=jnp.int32))
    pos_s = past + jnp.arange(t_s, dtype=jnp.int32)
    cos_s, sin_s = _head_block_tables(jnp.tile(pos_s, tm // t_s))
    cos_k, sin_k = _rope_angles(jnp.arange(past + PAGE, dtype=jnp.int32))
    cc_tab = jnp.concatenate([cos_k, cos_k], axis=1)
    ss_tab = jnp.concatenate([sin_k, sin_k], axis=1)

    zero_a = jnp.zeros((b_p, HIST_A, D_A), F32)
    zero_b = jnp.zeros((b_p, HIST_B, CONV_B_DIM), F32)
    zero_h = jnp.zeros((b_p, SSM_HEADS, SSM_HEAD_DIM, SSM_STATE), F32)

    y_p = x_prompt.reshape(b_p * t_p, D_MODEL)
    y_s = x_sample.reshape(b_s * t_s, D_MODEL)
    outs_p = ([], [], [], [], [])
    outs_s = ([], [], [], [], [])
    for l in range(depth):
        (glu, sa, sz, xbc, dtv, q, k, v, lat, kr, sc, g) = _in_proj(y_p, wp, l, cos_p, sin_p, False, tm)
        ya, new_a = _conv_a(glu, sa, zero_a, wp, l, b_p, 256)
        yb, new_b, h_fin = _ssd_prompt(xbc, dtv, sz, zero_b, zero_h, wp, l, b_p)
        att = _attn_prompt(q, k, v, b_p, t_p)
        y_p = _out_proj(y_p, ya, yb, att, sc, g, wp, l, tm)
        for lst, val in zip(outs_p, (new_a, new_b, h_fin, lat.reshape(b_p, t_p, KV_LORA),
                                     kr.reshape(b_p, t_p, ROPE_DIM))):
            lst.append(val)

        (glu, sa, sz, xbc, dtv, q, k, v, lat, kr, sc, g, qabs) = _in_proj(y_s, wp, l, cos_s, sin_s, True, tm)
        ya, new_a = _conv_a(glu, sa, state_conv_a[l], wp, l, b_s, t_s)
        yb, new_b, h_fin = _ssd_sample(xbc, dtv, sz, state_conv_b[l], state_ssm[l], wp, l, b_s, t_s)
        q4 = q.reshape(b_s, t_s * MLA_HEADS, HEAD_BLOCK)
        q1 = q4[:, :, NOPE_DIM:NOPE_DIM + HALF_ROPE]
        q2 = q4[:, :, NOPE_DIM + HALF_ROPE:QK_DIM]
        qc = jnp.concatenate([q1, q2], axis=-1)
        qs = jnp.concatenate([q2, -q1], axis=-1)
        ctx = _attn_sample(page_table, qabs.reshape(b_s, Q_ROWS, KV_LORA), qc, qs,
                           lat.reshape(b_s, t_s, KV_LORA), kr.reshape(b_s, t_s, ROPE_DIM),
                           cache_latent, cache_krope, cc_tab, ss_tab, wp, l)
        att = _uv_proj(ctx.reshape(b_s * t_s, MLA_HEADS * KV_LORA), wp, l, tm)
        y_s = _out_proj(y_s, ya, yb, att, sc, g, wp, l, tm)
        for lst, val in zip(outs_s, (new_a, new_b, h_fin, lat.reshape(b_s, t_s, KV_LORA),
                                     kr.reshape(b_s, t_s, ROPE_DIM))):
            lst.append(val)

    stack = lambda lists: [jnp.stack(v, axis=0) for v in lists]
    return (y_p.reshape(b_p, t_p, D_MODEL), y_s.reshape(b_s, t_s, D_MODEL), *stack(outs_p), *stack(outs_s))
```

```python
import functools

import jax
import jax.numpy as jnp
from jax import lax
from jax.experimental import pallas as pl
from jax.experimental.pallas import tpu as pltpu

F32 = jnp.float32
BF16 = jnp.bfloat16
HIGHEST = lax.Precision.HIGHEST

D_MODEL = 1024
D_A = 512
CONV_A_WIDTH = 31
D_B = 1024
SSM_HEAD_DIM = 64
SSM_HEADS = 16
SSM_GROUPS = 2
SSM_STATE = 128
CONV_B_WIDTH = 4
CONV_B_DIM = D_B + 2 * SSM_GROUPS * SSM_STATE
MLA_HEADS = 8
NOPE_DIM = 64
ROPE_DIM = 32
HALF_ROPE = ROPE_DIM // 2
QK_DIM = NOPE_DIM + ROPE_DIM
V_DIM = 64
D_C = MLA_HEADS * V_DIM
Q_LORA = 384
KV_LORA = 256
ROPE_BASE = 10000.0
EPS = 1e-6
N_BRANCH = 3
PAGE = 128

LANES = 128
SUBLANES = 8
HEAD_BLOCK = LANES
QK_PAD = MLA_HEADS * HEAD_BLOCK
W1_COLS = 3 * D_A + D_B + CONV_B_DIM
OFF_CQ = 0
OFF_CKV = OFF_CQ + Q_LORA
OFF_CS = OFF_CKV + KV_LORA
OFF_MG = OFF_CS + D_C
OFF_KR = OFF_MG + N_BRANCH * D_MODEL
OFF_DT = OFF_KR + LANES
W2_COLS = OFF_DT + LANES

VMEM_LIMIT = 56 * 1024 * 1024


def _dot(a, b):
    return jnp.dot(a.astype(BF16), b.astype(BF16), preferred_element_type=F32)


def _dot_nt(a, b):
    return lax.dot_general(a.astype(BF16), b.astype(BF16), (((1,), (1,)), ((), ())),
                           preferred_element_type=F32)


def _sigmoid(x):
    return 1.0 / (1.0 + jnp.exp(-x))


def _silu(x):
    return x * _sigmoid(x)


def _rms(x, w):
    return x * lax.rsqrt(jnp.mean(x * x, axis=-1, keepdims=True) + EPS) * w


def _expand_heads(x, rows):
    lane = lax.broadcasted_iota(jnp.int32, (rows, LANES), 1)
    blocks = []
    for i in range(SSM_HEADS // 2):
        blocks.append(jnp.where(lane < SSM_HEAD_DIM, x[:, 2 * i:2 * i + 1], x[:, 2 * i + 1:2 * i + 2]))
    return jnp.concatenate(blocks, axis=1)


def _const_spec(shape, layer=None):
    if layer is None:
        return pl.BlockSpec(shape, lambda *_: (0,) * len(shape), pipeline_mode=pl.Buffered(1))
    return pl.BlockSpec((None,) + shape, lambda *_: (layer,) + (0,) * len(shape),
                        pipeline_mode=pl.Buffered(1))


def _params(n_axes):
    return pltpu.CompilerParams(dimension_semantics=("arbitrary",) * n_axes,
                                vmem_limit_bytes=VMEM_LIMIT)


def _norm_rope_block(val, w_row, cosb, sinb, lane):
    ss = jnp.sum(val * val, axis=-1, keepdims=True)
    vn = val * lax.rsqrt(ss * (1.0 / QK_DIM) + EPS) * w_row
    sw = jnp.where(lane < NOPE_DIM + HALF_ROPE,
                   pltpu.roll(vn, LANES - HALF_ROPE, 1), pltpu.roll(vn, HALF_ROPE, 1))
    return vn * cosb + sw * sinb


def _in_kernel(with_abs, x_ref, nw_ref, w1_ref, w2_ref, dtb_ref, qaw_ref, wuq_ref, kvw_ref, wuk_ref,
               wuv_ref, qnw_ref, knw_ref, cos_ref, sin_ref, *rest):
    if with_abs:
        knw_nope_ref, wabs_ref = rest[:2]
        rest = rest[2:]
    glu_ref, sa_ref, sz_ref, xbc_ref, dtv_ref, q_ref, lat_ref, kr_ref, sc_ref, g_ref = rest[:10]
    if with_abs:
        qabs_ref = rest[10]
    else:
        k_ref, v_ref = rest[10:12]
    tm = x_ref.shape[0]
    x = x_ref[...]
    hb = _rms(x, nw_ref[...]).astype(BF16)

    a_val = _dot(hb, w1_ref[:, 0:D_A])
    a_gate = _dot(hb, w1_ref[:, D_A:2 * D_A])
    glu_ref[...] = a_val * _sigmoid(a_gate)
    sa_ref[...] = _silu(_dot(hb, w1_ref[:, 2 * D_A:3 * D_A]))
    sz_ref[...] = _silu(_dot(hb, w1_ref[:, 3 * D_A:3 * D_A + D_B]))
    xbc_ref[...] = _dot(hb, w1_ref[:, 3 * D_A + D_B:W1_COLS])

    cq = _dot(hb, w2_ref[:, OFF_CQ:OFF_CKV])
    ckv = _dot(hb, w2_ref[:, OFF_CKV:OFF_CS])
    sc_ref[...] = _silu(_dot(hb, w2_ref[:, OFF_CS:OFF_MG]))
    g_ref[...] = _sigmoid(_dot(hb, w2_ref[:, OFF_MG:OFF_KR]))
    krp = _dot(hb, w2_ref[:, OFF_KR:OFF_DT])
    dtp = _dot(hb, w2_ref[:, OFF_DT:W2_COLS])
    kr_ref[...] = krp[:, NOPE_DIM:QK_DIM]
    dtx = dtp + dtb_ref[...]
    dtv_ref[...] = jnp.maximum(dtx, 0.0) + jnp.log1p(jnp.exp(-jnp.abs(dtx)))

    q = _dot(_rms(cq, qaw_ref[...]), wuq_ref[...])
    latn = _rms(ckv, kvw_ref[...])
    lat_ref[...] = latn
    latb = latn.astype(BF16)
    if not with_abs:
        kn = _dot(latb, wuk_ref[...])
        for p in range(MLA_HEADS // 2):
            v_ref[p] = _dot(latb, wuv_ref[:, p * LANES:(p + 1) * LANES]).astype(BF16)

    cosb = cos_ref[...]
    sinb = sin_ref[...]
    lane = lax.broadcasted_iota(jnp.int32, (tm, LANES), 1)
    scale = QK_DIM ** -0.5
    for h in range(MLA_HEADS):
        sl = slice(h * HEAD_BLOCK, (h + 1) * HEAD_BLOCK)
        qr = _norm_rope_block(q[:, sl], qnw_ref[...], cosb, sinb, lane) * scale
        q_ref[h] = qr.astype(BF16)
        if with_abs:
            qa = _dot(qr * knw_nope_ref[...], wabs_ref[sl, :])
            qabs_ref[:, h * KV_LORA:(h + 1) * KV_LORA] = qa.astype(BF16)
        else:
            kr = _norm_rope_block(kn[:, sl] + krp, knw_ref[...], cosb, sinb, lane)
            k_ref[h] = kr.astype(BF16)


def _in_proj(x, wp, layer, cos_tab, sin_tab, with_abs, tm=256):
    n = x.shape[0]
    assert n % tm == 0 and cos_tab.shape[0] % tm == 0
    nper = cos_tab.shape[0] // tm
    row = lambda w: pl.BlockSpec((tm, w), lambda i: (i, 0))
    tab = pl.BlockSpec((tm, LANES), lambda i: (i % nper, 0))
    in_specs = [
        row(D_MODEL),
        _const_spec((1, D_MODEL), layer),
        _const_spec((D_MODEL, W1_COLS), layer),
        _const_spec((D_MODEL, W2_COLS), layer),
        _const_spec((1, LANES), layer),
        _const_spec((1, Q_LORA), layer),
        _const_spec((Q_LORA, QK_PAD), layer),
        _const_spec((1, KV_LORA), layer),
        _const_spec((KV_LORA, QK_PAD), layer),
        _const_spec((KV_LORA, D_C), layer),
        _const_spec((1, LANES), layer),
        _const_spec((1, LANES), layer),
        tab, tab,
    ]
    args = [x, wp['norm_w'], wp['w1'], wp['w2'], wp['dt_bias'], wp['q_a_norm_w'], wp['w_uq'],
            wp['kv_a_norm_w'], wp['w_uk'], wp['w_uv'], wp['q_norm_w'], wp['k_norm_w'], cos_tab, sin_tab]
    heads = lambda nh: (jax.ShapeDtypeStruct((nh, n, LANES), BF16),
                        pl.BlockSpec((nh, tm, LANES), lambda i: (0, i, 0)))
    flat = lambda w, dt: (jax.ShapeDtypeStruct((n, w), dt), row(w))
    outs = [flat(D_A, F32), flat(D_A, F32), flat(D_B, F32), flat(CONV_B_DIM, F32), flat(LANES, F32),
            heads(MLA_HEADS), flat(KV_LORA, F32), flat(ROPE_DIM, F32), flat(D_C, F32),
            flat(N_BRANCH * D_MODEL, F32)]
    if with_abs:
        in_specs += [_const_spec((1, LANES), layer), _const_spec((QK_PAD, KV_LORA), layer)]
        args += [wp['k_norm_w_nope'], wp['w_uk_abs']]
        outs.append(flat(MLA_HEADS * KV_LORA, BF16))
    else:
        outs += [heads(MLA_HEADS), heads(MLA_HEADS // 2)]
    return pl.pallas_call(
        functools.partial(_in_kernel, with_abs),
        out_shape=[o[0] for o in outs],
        grid=(n // tm,),
        in_specs=in_specs,
        out_specs=[o[1] for o in outs],
        compiler_params=_params(1),
    )(*args)


HIST_A = CONV_A_WIDTH - 1
HIST_A_PAD = 32


def _conv_a_kernel(glu_ref, sa_ref, hist_ref, cw_ref, cb_ref, lw_ref, lb_ref, ya_ref, newa_ref, buf):
    t = pl.program_id(1)
    tt = glu_ref.shape[0]
    lo = HIST_A_PAD - HIST_A

    @pl.when(t == 0)
    def _():
        buf[lo:HIST_A_PAD, :] = hist_ref[0]
        buf[HIST_A_PAD + tt:HIST_A_PAD + tt + SUBLANES, :] = jnp.zeros((SUBLANES, D_A), F32)

    buf[HIST_A_PAD:HIST_A_PAD + tt, :] = glu_ref[...]
    acc = jnp.broadcast_to(cb_ref[...], (tt, D_A))
    for r in range(SUBLANES):
        part = None
        for a in range((lo + CONV_A_WIDTH - 1) // SUBLANES + 1):
            k = SUBLANES * a + r - lo
            if 0 <= k < CONV_A_WIDTH:
                term = cw_ref[k:k + 1, :] * buf[SUBLANES * a:SUBLANES * a + tt + SUBLANES, :]
                part = term if part is None else part + term
        acc = acc + part[r:r + tt, :]
    xc = acc - jnp.mean(acc, axis=-1, keepdims=True)
    ln = xc * lax.rsqrt(jnp.mean(xc * xc, axis=-1, keepdims=True) + EPS) * lw_ref[...] + lb_ref[...]
    ya_ref[...] = _silu(ln) * sa_ref[...]
    tail = buf[tt + lo:tt + HIST_A_PAD, :]

    @pl.when(t == pl.num_programs(1) - 1)
    def _():
        newa_ref[0] = tail

    buf[lo:HIST_A_PAD, :] = tail


def _conv_a(glu, sa, hist, wp, layer, batch, tt):
    n = glu.shape[0]
    nt = n // (batch * tt)
    row = pl.BlockSpec((tt, D_A), lambda b, t: (b * nt + t, 0))
    return pl.pallas_call(
        _conv_a_kernel,
        out_shape=[jax.ShapeDtypeStruct((n, D_A), F32),
                   jax.ShapeDtypeStruct((batch, HIST_A, D_A), F32)],
        grid=(batch, nt),
        in_specs=[row, row,
                  pl.BlockSpec((1, HIST_A, D_A), lambda b, t: (b, 0, 0)),
                  _const_spec((CONV_A_WIDTH, D_A), layer),
                  _const_spec((1, D_A), layer), _const_spec((1, D_A), layer),
                  _const_spec((1, D_A), layer)],
        out_specs=[row, pl.BlockSpec((1, HIST_A, D_A), lambda b, t: (b, 0, 0))],
        scratch_shapes=[pltpu.VMEM((HIST_A_PAD + tt + SUBLANES, D_A), F32)],
        compiler_params=_params(2),
    )(glu, sa, hist, wp['conv_a_w'], wp['conv_a_b'], wp['ln_a_w'], wp['ln_a_b'])


HIST_B = CONV_B_WIDTH - 1
HIST_B_PAD = 8
GROUP_COLS = D_B // SSM_GROUPS
HEADS_PER_GROUP = SSM_HEADS // SSM_GROUPS


def _conv_b(cbuf, xbc_ref, hist_loader, cw_ref, cb_ref, q, first):
    lo = HIST_B_PAD - HIST_B

    if first is None:
        cbuf[lo:HIST_B_PAD, :] = hist_loader()
    else:
        @pl.when(first)
        def _():
            cbuf[lo:HIST_B_PAD, :] = hist_loader()

    cbuf[HIST_B_PAD:HIST_B_PAD + q, :] = xbc_ref[...]
    acc = jnp.broadcast_to(cb_ref[...], (q, CONV_B_DIM))
    for k in range(CONV_B_WIDTH):
        acc = acc + cw_ref[k:k + 1, :] * cbuf[lo + k:lo + k + q, :]
    tail = cbuf[q + lo:q + HIST_B_PAD, :]
    return _silu(acc), tail


def _ssd_p_kernel(xbc_ref, hist_ref, cw_ref, cb_ref, dtv_ref, a_ref, dsk_ref, sz_ref, nbw_ref, h0_ref,
                  yb_ref, newb_ref, hfin_ref, cbuf, ht):
    c = pl.program_id(1)
    last = c == pl.num_programs(1) - 1
    q = xbc_ref.shape[0]
    xbc, tail = _conv_b(cbuf, xbc_ref, lambda: hist_ref[0], cw_ref, cb_ref, q, c == 0)

    @pl.when(c == 0)
    def _():
        ht[...] = h0_ref[0].reshape(SSM_HEADS * SSM_HEAD_DIM, SSM_STATE).T

    @pl.when(last)
    def _():
        newb_ref[0] = tail

    cbuf[HIST_B_PAD - HIST_B:HIST_B_PAD, :] = tail

    xs = xbc[:, :D_B]
    bs = xbc[:, D_B:D_B + SSM_GROUPS * SSM_STATE]
    cs = xbc[:, D_B + SSM_GROUPS * SSM_STATE:]
    dt = dtv_ref[...]
    adt = dt * a_ref[...]
    row = lax.broadcasted_iota(jnp.int32, (q, q), 0)
    col = lax.broadcasted_iota(jnp.int32, (q, q), 1)
    tril = row >= col
    acum = jnp.dot(tril.astype(F32), adt, precision=HIGHEST, preferred_element_type=F32)
    acum_t = acum.T
    dt_t = dt.T
    alast = acum[q - 1:q, :]
    dec_end = jnp.exp(alast - acum) * dt
    xsb = xs.astype(BF16)
    xw = (xs * _expand_heads(dec_end, q)).astype(BF16)
    chunk_decay = _expand_heads(jnp.exp(alast), 1)
    lane4 = lax.broadcasted_iota(jnp.int32, (q, 4 * SSM_HEAD_DIM), 1) // SSM_HEAD_DIM

    y_diag = []
    y_off = []
    for g in range(SSM_GROUPS):
        bsg = bs[:, g * SSM_STATE:(g + 1) * SSM_STATE]
        csg = cs[:, g * SSM_STATE:(g + 1) * SSM_STATE]
        cb = _dot_nt(csg, bsg)
        for quad in range(HEADS_PER_GROUP // 4):
            ws = []
            rhs = []
            qd = g * (HEADS_PER_GROUP // 4) + quad
            xblk = xsb[:, qd * 4 * SSM_HEAD_DIM:(qd + 1) * 4 * SSM_HEAD_DIM]
            for hh in range(4):
                h = qd * 4 + hh
                seg = acum[:, h:h + 1] - acum_t[h:h + 1, :]
                lm = jnp.exp(jnp.where(tril, seg, -jnp.inf))
                ws.append((cb * lm * dt_t[h:h + 1, :]).astype(BF16))
                rhs.append(jnp.where(lane4 == hh, xblk, jnp.zeros_like(xblk)))
            y_diag.append(jnp.dot(jnp.concatenate(ws, axis=1), jnp.concatenate(rhs, axis=0),
                                  preferred_element_type=F32))
        gs = slice(g * GROUP_COLS, (g + 1) * GROUP_COLS)
        htg = ht[:, gs]
        y_off.append(_dot(csg, htg))
        ht[:, gs] = htg * chunk_decay[:, gs] + _dot(bsg.T, xw[:, gs])

    y = (jnp.concatenate(y_diag, axis=1)
         + jnp.concatenate(y_off, axis=1) * _expand_heads(jnp.exp(acum), q)
         + dsk_ref[...] * xs)
    yb_ref[...] = _rms(y * sz_ref[...], nbw_ref[...])

    @pl.when(last)
    def _():
        hfin_ref[0] = ht[...].T.reshape(SSM_HEADS, SSM_HEAD_DIM, SSM_STATE)


def _ssd_prompt(xbc, dtv, sz, hist, h0, wp, layer, batch, q=128):
    n = xbc.shape[0]
    nc = n // (batch * q)
    row = lambda w: pl.BlockSpec((q, w), lambda b, c: (b * nc + c, 0))
    state = pl.BlockSpec((1, SSM_HEADS, SSM_HEAD_DIM, SSM_STATE), lambda b, c: (b, 0, 0, 0))
    histb = pl.BlockSpec((1, HIST_B, CONV_B_DIM), lambda b, c: (b, 0, 0))
    return pl.pallas_call(
        _ssd_p_kernel,
        out_shape=[jax.ShapeDtypeStruct((n, D_B), F32),
                   jax.ShapeDtypeStruct((batch, HIST_B, CONV_B_DIM), F32),
                   jax.ShapeDtypeStruct((batch, SSM_HEADS, SSM_HEAD_DIM, SSM_STATE), F32)],
        grid=(batch, nc),
        in_specs=[row(CONV_B_DIM), histb,
                  _const_spec((CONV_B_WIDTH, CONV_B_DIM), layer), _const_spec((1, CONV_B_DIM), layer),
                  row(LANES), _const_spec((1, LANES), layer), _const_spec((1, D_B), layer),
                  row(D_B), _const_spec((1, D_B), layer), state],
        out_specs=[row(D_B), histb, state],
        scratch_shapes=[pltpu.VMEM((HIST_B_PAD + q, CONV_B_DIM), F32),
                        pltpu.VMEM((SSM_STATE, D_B), F32)],
        compiler_params=_params(2),
    )(xbc, hist, wp['conv_b_w'], wp['conv_b_b'], dtv, wp['a_neg'], wp['d_skip'], sz, wp['norm_b_w'], h0)


def _ssd_s_kernel(xbc_ref, hist_ref, cw_ref, cb_ref, dtv_ref, a_ref, dsk_ref, sz_ref, nbw_ref, h0_ref,
                  yb_ref, newb_ref, hfin_ref, cbuf):
    q = xbc_ref.shape[0]
    assert q * SSM_HEADS == LANES
    xbc, tail = _conv_b(cbuf, xbc_ref, lambda: hist_ref[0], cw_ref, cb_ref, q, None)
    newb_ref[0] = tail
    xs = xbc[:, :D_B]
    bs = xbc[:, D_B:D_B + SSM_GROUPS * SSM_STATE]
    cs = xbc[:, D_B + SSM_GROUPS * SSM_STATE:]
    dt = dtv_ref[...]
    adt = dt * a_ref[...]
    row = lax.broadcasted_iota(jnp.int32, (q, LANES), 0)
    lane = lax.broadcasted_iota(jnp.int32, (q, LANES), 1)
    tril8 = (row >= lane).astype(F32)[:, :q]
    acum = jnp.dot(tril8, adt, precision=HIGHEST, preferred_element_type=F32)
    er = lax.broadcasted_iota(jnp.int32, (LANES, LANES), 0)
    ec = lax.broadcasted_iota(jnp.int32, (LANES, LANES), 1)
    rep = (er == ec // q).astype(F32)
    a_col = jnp.dot(acum, rep, precision=HIGHEST, preferred_element_type=F32)
    d_col = jnp.dot(dt, rep, precision=HIGHEST, preferred_element_type=F32)
    sel = row == lane % q
    a_row = jnp.sum(jnp.where(sel, a_col, 0.0), axis=0, keepdims=True)
    d_row = jnp.sum(jnp.where(sel, d_col, 0.0), axis=0, keepdims=True)
    lm = jnp.exp(jnp.where(row >= lane % q, a_col - a_row, -jnp.inf))
    br = lax.broadcasted_iota(jnp.int32, (LANES, SSM_GROUPS * SSM_STATE), 0)
    bc = lax.broadcasted_iota(jnp.int32, (LANES, SSM_GROUPS * SSM_STATE), 1)
    bs_t = jnp.where(br // (q * HEADS_PER_GROUP) == bc // SSM_STATE,
                     jnp.concatenate([bs] * SSM_HEADS, axis=0), 0.0)
    cb_flat = _dot_nt(cs, bs_t)
    w = (cb_flat * lm * d_row).astype(BF16)
    xr = lax.broadcasted_iota(jnp.int32, (LANES, D_B), 0)
    xc = lax.broadcasted_iota(jnp.int32, (LANES, D_B), 1)
    rhs = jnp.where(xr // q == xc // SSM_HEAD_DIM, jnp.concatenate([xs] * SSM_HEADS, axis=0), 0.0)
    y_diag = jnp.dot(w, rhs.astype(BF16), preferred_element_type=F32)

    h0 = h0_ref[0].reshape(SSM_HEADS * SSM_HEAD_DIM, SSM_STATE)
    alast = acum[q - 1:q, :]
    ea_last = jnp.exp(alast)
    dec_end = jnp.exp(alast - acum) * dt
    xw = (xs * _expand_heads(dec_end, q)).astype(BF16)
    ir = lax.broadcasted_iota(jnp.int32, (GROUP_COLS, GROUP_COLS), 0)
    ic = lax.broadcasted_iota(jnp.int32, (GROUP_COLS, GROUP_COLS), 1)
    eye = (ir == ic).astype(BF16)
    y_off = []
    for g in range(SSM_GROUPS):
        gs = slice(g * GROUP_COLS, (g + 1) * GROUP_COLS)
        bsg = bs[:, g * SSM_STATE:(g + 1) * SSM_STATE]
        csg = cs[:, g * SSM_STATE:(g + 1) * SSM_STATE]
        y_off.append(_dot_nt(csg, h0[gs, :]))
        xw_t = _dot_nt(eye, xw[:, gs])
        st = _dot(xw_t, bsg)
        for hh in range(HEADS_PER_GROUP):
            h = g * HEADS_PER_GROUP + hh
            rs = slice(hh * SSM_HEAD_DIM, (hh + 1) * SSM_HEAD_DIM)
            hfin_ref[0, h] = h0_ref[0, h] * ea_last[:, h:h + 1] + st[rs, :]
    y = (y_diag + jnp.concatenate(y_off, axis=1) * _expand_heads(jnp.exp(acum), q)
         + dsk_ref[...] * xs)
    yb_ref[...] = _rms(y * sz_ref[...], nbw_ref[...])


def _ssd_sample(xbc, dtv, sz, hist, h0, wp, layer, batch, q):
    n = xbc.shape[0]
    row = lambda w: pl.BlockSpec((q, w), lambda b: (b, 0))
    state = pl.BlockSpec((1, SSM_HEADS, SSM_HEAD_DIM, SSM_STATE), lambda b: (b, 0, 0, 0))
    state_in = pl.BlockSpec((None, 1, SSM_HEADS, SSM_HEAD_DIM, SSM_STATE), lambda b: (layer, b, 0, 0, 0))
    histb = pl.BlockSpec((1, HIST_B, CONV_B_DIM), lambda b: (b, 0, 0))
    return pl.pallas_call(
        _ssd_s_kernel,
        out_shape=[jax.ShapeDtypeStruct((n, D_B), F32),
                   jax.ShapeDtypeStruct((batch, HIST_B, CONV_B_DIM), F32),
                   jax.ShapeDtypeStruct((batch, SSM_HEADS, SSM_HEAD_DIM, SSM_STATE), F32)],
        grid=(batch,),
        in_specs=[row(CONV_B_DIM), histb,
                  _const_spec((CONV_B_WIDTH, CONV_B_DIM), layer), _const_spec((1, CONV_B_DIM), layer),
                  row(LANES), _const_spec((1, LANES), layer), _const_spec((1, D_B), layer),
                  row(D_B), _const_spec((1, D_B), layer), state_in],
        out_specs=[row(D_B), histb, state],
        scratch_shapes=[pltpu.VMEM((HIST_B_PAD + q, CONV_B_DIM), F32)],
        compiler_params=_params(1),
    )(xbc, hist, wp['conv_b_w'], wp['conv_b_b'], dtv, wp['a_neg'], wp['d_skip'], sz, wp['norm_b_w'], h0)


def _attn_p_kernel(tq, q_ref, k_ref, v_ref, o_ref):
    seq = q_ref.shape[1]
    row = lax.broadcasted_iota(jnp.int32, (tq, tq), 0)
    col = lax.broadcasted_iota(jnp.int32, (tq, tq), 1)
    causal = row >= col
    lane = lax.broadcasted_iota(jnp.int32, (tq, LANES), 1)
    for i in range(seq // tq):
        lo = i * tq
        rows = slice(lo, lo + tq)

        def pair(p, carry, lo=lo, rows=rows):
            outs = []
            for e in range(2):
                h = 2 * p + e
                qh = q_ref[h, rows, :]
                s_d = jnp.where(causal, _dot_nt(qh, k_ref[h, rows, :]), -jnp.inf)
                m = jnp.max(s_d, axis=-1, keepdims=True)
                if lo:
                    s_p = _dot_nt(qh, k_ref[h, 0:lo, :])
                    m = jnp.maximum(m, jnp.max(s_p, axis=-1, keepdims=True))
                p_d = jnp.exp(s_d - m)
                l = jnp.sum(p_d, axis=-1, keepdims=True)
                pv = jnp.dot(p_d.astype(BF16), v_ref[p, rows, :], preferred_element_type=F32)
                if lo:
                    p_p = jnp.exp(s_p - m)
                    l = l + jnp.sum(p_p, axis=-1, keepdims=True)
                    pv = pv + jnp.dot(p_p.astype(BF16), v_ref[p, 0:lo, :], preferred_element_type=F32)
                outs.append(pv * (1.0 / l))
            o_ref[p, rows, :] = jnp.where(lane < V_DIM, outs[0], outs[1])
            return carry

        lax.fori_loop(0, MLA_HEADS // 2, pair, 0)


def _attn_prompt(q, k, v, batch, seq, tq=256):
    n = q.shape[1]
    head_rows = lambda nh: pl.BlockSpec((nh, seq, LANES), lambda b: (0, b, 0))
    return pl.pallas_call(
        functools.partial(_attn_p_kernel, tq),
        out_shape=jax.ShapeDtypeStruct((MLA_HEADS // 2, n, LANES), F32),
        grid=(batch,),
        in_specs=[head_rows(MLA_HEADS), head_rows(MLA_HEADS), head_rows(MLA_HEADS // 2)],
        out_specs=head_rows(MLA_HEADS // 2),
        compiler_params=_params(1),
    )(q, k, v)


PAGES_PER_CHUNK = 16
CHUNK = PAGES_PER_CHUNK * PAGE
Q_ROWS = 64
LAT_SLOTS = 2
KR_SLOTS = 3


def _attn_s_kernel(layer, n_chunks, past, pt_ref, qn_ref, qc_ref, wukt_ref, knwr_ref, cos_ref, sin_ref,
                   latn_ref, krn_ref, clat_hbm, ckr_hbm, ctx_ref,
                   latbuf, krbuf, sems, latb_scr, r_scr, newlat, newkr):
    b = pl.program_id(0)
    total = pl.num_programs(0) * n_chunks
    t_new = latn_ref.shape[1]

    def copies(g):
        bb = g // n_chunks
        cc = g % n_chunks
        ls = g % LAT_SLOTS
        ks = g % KR_SLOTS
        out = []
        for k in range(PAGES_PER_CHUNK):
            page = pt_ref[bb, cc * PAGES_PER_CHUNK + k]
            keys = pl.ds(k * PAGE, PAGE)
            out.append(pltpu.make_async_copy(clat_hbm.at[layer, page], latbuf.at[ls, keys],
                                             sems.at[0, ls]))
            out.append(pltpu.make_async_copy(ckr_hbm.at[layer, page], krbuf.at[ks, :, keys],
                                             sems.at[1, ks]))
        return out

    def norm_factor(latb, krt):
        ck = latb.shape[0]
        knt = _dot_nt(wukt_ref[...], latb)
        ssq = jnp.sum((knt * knt).reshape(MLA_HEADS, NOPE_DIM, ck), axis=1)
        ssr = jnp.sum(krt * krt, axis=0, keepdims=True)
        return lax.rsqrt((ssq + ssr) * (1.0 / QK_DIM) + EPS)

    def stage1(g):
        ls = g % LAT_SLOTS
        latb = latbuf[ls].astype(BF16)
        latb_scr[ls] = latb
        r_scr[ls] = norm_factor(latb, krbuf[g % KR_SLOTS])

    def stage2(latb, krt, r, cos, sin, carry, mask):
        m, l, acc = carry
        x = krt * knwr_ref[...]
        x1 = x[0:HALF_ROPE, :]
        x2 = x[HALF_ROPE:ROPE_DIM, :]
        kro = jnp.concatenate([x1 * cos - x2 * sin, x2 * cos + x1 * sin], axis=0)
        s = _dot_nt(qn_ref[0], latb) + _dot(qc_ref[0], kro)
        s = s * jnp.concatenate([r] * (Q_ROWS // MLA_HEADS), axis=0)
        if mask is not None:
            s = jnp.where(mask, s, -1e30)
        m_new = jnp.maximum(m, jnp.max(s, axis=-1, keepdims=True))
        alpha = jnp.exp(m - m_new)
        pr = jnp.exp(s - m_new)
        if mask is not None:
            pr = jnp.where(mask, pr, 0.0)
        l = alpha * l + jnp.sum(pr, axis=-1, keepdims=True)
        acc = alpha * acc + jnp.dot(pr.astype(BF16), latb, preferred_element_type=F32)
        return m_new, l, acc

    @pl.when(b == 0)
    def _():
        for g0 in range(2):
            for cp in copies(g0):
                cp.start()
        for cp in copies(0):
            cp.wait()
        stage1(0)
        newlat[...] = jnp.zeros_like(newlat)
        newkr[...] = jnp.zeros_like(newkr)

    def body(c, carry):
        g = b * n_chunks + c

        @pl.when(g + 1 < total)
        def _():
            for cp in copies(g + 1):
                cp.wait()

        @pl.when(g + 2 < total)
        def _():
            for cp in copies(g + 2):
                cp.start()

        stage1(g + 1)
        ls = g % LAT_SLOTS
        keys = pl.ds(pl.multiple_of(c * CHUNK, CHUNK), CHUNK)
        return stage2(latb_scr[ls], krbuf[g % KR_SLOTS], r_scr[ls], cos_ref[:, keys], sin_ref[:, keys],
                      carry, None)

    init = (jnp.full((Q_ROWS, 1), -1e30, F32), jnp.zeros((Q_ROWS, 1), F32),
            jnp.zeros((Q_ROWS, KV_LORA), F32))
    carry = lax.fori_loop(0, n_chunks, body, init)

    newlat[0:t_new, :] = latn_ref[0]
    newkr[:, 0:t_new] = krn_ref[0]
    qi = lax.broadcasted_iota(jnp.int32, (Q_ROWS, PAGE), 0) // MLA_HEADS
    kj = lax.broadcasted_iota(jnp.int32, (Q_ROWS, PAGE), 1)
    latb = newlat[...].astype(BF16)
    krt = newkr[...]
    _, l, acc = stage2(latb, krt, norm_factor(latb, krt), cos_ref[:, past:past + PAGE],
                       sin_ref[:, past:past + PAGE], carry, kj <= qi)
    ctx_ref[0] = acc * (1.0 / l)


def _attn_sample(page_table, qn, qc, latn, krn_t, cache_latent, cache_krope_t, cos_t, sin_t, wp, layer):
    batch, n_pages = page_table.shape
    assert n_pages % PAGES_PER_CHUNK == 0
    n_chunks = n_pages // PAGES_PER_CHUNK
    past = n_pages * PAGE
    t_new = latn.shape[1]
    per_seq = lambda shape: pl.BlockSpec((1,) + shape, lambda b, pt: (b, 0, 0))
    grid_spec = pltpu.PrefetchScalarGridSpec(
        num_scalar_prefetch=1,
        grid=(batch,),
        in_specs=[per_seq((Q_ROWS, KV_LORA)), per_seq((Q_ROWS, ROPE_DIM)),
                  _const_spec((MLA_HEADS * NOPE_DIM, KV_LORA), layer),
                  _const_spec((ROPE_DIM, 1), layer),
                  _const_spec(cos_t.shape), _const_spec(sin_t.shape),
                  per_seq((t_new, KV_LORA)), per_seq((ROPE_DIM, t_new)),
                  pl.BlockSpec(memory_space=pl.ANY), pl.BlockSpec(memory_space=pl.ANY)],
        out_specs=per_seq((Q_ROWS, KV_LORA)),
        scratch_shapes=[pltpu.VMEM((LAT_SLOTS, CHUNK, KV_LORA), F32),
                        pltpu.VMEM((KR_SLOTS, ROPE_DIM, CHUNK), F32),
                        pltpu.SemaphoreType.DMA((2, KR_SLOTS)),
                        pltpu.VMEM((LAT_SLOTS, CHUNK, KV_LORA), BF16),
                        pltpu.VMEM((LAT_SLOTS, MLA_HEADS, CHUNK), F32),
                        pltpu.VMEM((PAGE, KV_LORA), F32),
                        pltpu.VMEM((ROPE_DIM, PAGE), F32)],
    )
    return pl.pallas_call(
        functools.partial(_attn_s_kernel, layer, n_chunks, past),
        out_shape=jax.ShapeDtypeStruct((batch, Q_ROWS, KV_LORA), F32),
        grid_spec=grid_spec,
        compiler_params=_params(1),
    )(page_table, qn, qc, wp['w_uk_t'], wp['k_norm_w_rope'], cos_t, sin_t, latn, krn_t,
      cache_latent, cache_krope_t)


def _uv_kernel(ctx_ref, wv_ref, att_ref):
    for p in range(MLA_HEADS // 2):
        acc = None
        for e in range(2):
            h = 2 * p + e
            part = _dot(ctx_ref[:, h * KV_LORA:(h + 1) * KV_LORA], wv_ref[h])
            acc = part if acc is None else acc + part
        att_ref[p] = acc


def _uv_proj(ctx, wp, layer, tm=256):
    n = ctx.shape[0]
    tm = min(tm, n)
    return pl.pallas_call(
        _uv_kernel,
        out_shape=jax.ShapeDtypeStruct((MLA_HEADS // 2, n, LANES), F32),
        grid=(n // tm,),
        in_specs=[pl.BlockSpec((tm, MLA_HEADS * KV_LORA), lambda i: (i, 0)),
                  _const_spec((MLA_HEADS, KV_LORA, LANES), layer)],
        out_specs=pl.BlockSpec((MLA_HEADS // 2, tm, LANES), lambda i: (0, i, 0)),
        compiler_params=_params(1),
    )(ctx, wp['w_uv_pair'])


def _out_kernel(x_ref, ya_ref, yb_ref, att_ref, sc_ref, g_ref, wa_ref, wb_ref, wc_ref, wo_ref, y_ref):
    br_a = _dot(ya_ref[...], wa_ref[...])
    br_b = _dot(yb_ref[...], wb_ref[...])
    att = jnp.concatenate([att_ref[p] for p in range(MLA_HEADS // 2)], axis=1)
    br_c = _dot(att * sc_ref[...], wc_ref[...])
    merged = (g_ref[:, 0:D_MODEL] * br_a + g_ref[:, D_MODEL:2 * D_MODEL] * br_b
              + g_ref[:, 2 * D_MODEL:3 * D_MODEL] * br_c)
    y_ref[...] = x_ref[...] + _dot(merged, wo_ref[...])


def _out_proj(x, ya, yb, att, sc, g, wp, layer, tm=256):
    n = x.shape[0]
    row = lambda w: pl.BlockSpec((tm, w), lambda i: (i, 0))
    return pl.pallas_call(
        _out_kernel,
        out_shape=jax.ShapeDtypeStruct((n, D_MODEL), F32),
        grid=(n // tm,),
        in_specs=[row(D_MODEL), row(D_A), row(D_B),
                  pl.BlockSpec((MLA_HEADS // 2, tm, LANES), lambda i: (0, i, 0)),
                  row(D_C), row(N_BRANCH * D_MODEL),
                  _const_spec((D_A, D_MODEL), layer), _const_spec((D_B, D_MODEL), layer),
                  _const_spec((D_C, D_MODEL), layer), _const_spec((D_MODEL, D_MODEL), layer)],
        out_specs=row(D_MODEL),
        compiler_params=_params(1),
    )(x, ya, yb, att, sc, g, wp['w_a_out'], wp['w_b_out'], wp['w_c_out'], wp['w_out'])


def _pad_lanes(row, width=LANES):
    return jnp.pad(row, ((0, 0), (0, 0), (0, width - row.shape[-1])))


def _prepare_weights(w_in, dt_bias, a_log, d_skip, w_uq, w_uk, w_uv, q_norm_w, k_norm_w, others):
    depth = w_in.shape[0]
    o = 0
    cuts = {}
    for name, width in (('a', 3 * D_A), ('z', D_B), ('xbc', CONV_B_DIM), ('dt', SSM_HEADS), ('cq', Q_LORA),
                        ('ckv', KV_LORA), ('kr', ROPE_DIM), ('cs', D_C), ('mg', N_BRANCH * D_MODEL)):
        cuts[name] = (o, o + width)
        o += width
    col = lambda name: w_in[:, :, cuts[name][0]:cuts[name][1]]
    zeros = lambda width: jnp.zeros((depth, D_MODEL, width), w_in.dtype)
    w1 = w_in[:, :, 0:W1_COLS].astype(BF16)
    w2 = jnp.concatenate([col('cq'), col('ckv'), col('cs'), col('mg'),
                          zeros(NOPE_DIM), col('kr'), zeros(LANES - QK_DIM),
                          col('dt'), zeros(LANES - SSM_HEADS)], axis=-1).astype(BF16)
    head_pad = lambda w: jnp.pad(w, ((0, 0), (0, 0), (0, 0), (0, HEAD_BLOCK - w.shape[-1])))
    wuq = head_pad(w_uq.reshape(depth, Q_LORA, MLA_HEADS, QK_DIM)).reshape(depth, Q_LORA, QK_PAD)
    wuk = head_pad(w_uk).reshape(depth, KV_LORA, QK_PAD)
    wuk_abs = jnp.transpose(head_pad(w_uk), (0, 2, 3, 1)).reshape(depth, QK_PAD, KV_LORA)
    wuk_t = jnp.transpose(w_uk, (0, 2, 3, 1)).reshape(depth, MLA_HEADS * NOPE_DIM, KV_LORA)
    wuv = w_uv.reshape(depth, KV_LORA, D_C)
    wv_h = jnp.transpose(w_uv, (0, 2, 1, 3))
    left = jnp.pad(wv_h, ((0, 0), (0, 0), (0, 0), (0, V_DIM)))
    right = jnp.pad(wv_h, ((0, 0), (0, 0), (0, 0), (V_DIM, 0)))
    even = (jnp.arange(MLA_HEADS) % 2 == 0)[None, :, None, None]
    wuv_pair = jnp.where(even, left, right)
    wp = dict(others)
    wp.update(
        w1=w1, w2=w2,
        dt_bias=_pad_lanes(dt_bias[:, None, :]),
        a_neg=_pad_lanes(-jnp.exp(a_log.astype(F32))[:, None, :]),
        d_skip=jnp.repeat(d_skip, SSM_HEAD_DIM, axis=-1)[:, None, :],
        w_uq=wuq.astype(BF16), w_uk=wuk.astype(BF16), w_uv=wuv.astype(BF16),
        w_uk_abs=wuk_abs.astype(BF16), w_uk_t=wuk_t.astype(BF16), w_uv_pair=wuv_pair.astype(BF16),
        q_norm_w=_pad_lanes(q_norm_w[:, None, :]), k_norm_w=_pad_lanes(k_norm_w[:, None, :]),
        k_norm_w_nope=_pad_lanes(k_norm_w[:, None, :NOPE_DIM]),
        k_norm_w_rope=k_norm_w[:, NOPE_DIM:, None],
    )
    return wp


def _rope_angles(pos):
    inv = jnp.power(ROPE_BASE, -jnp.arange(HALF_ROPE, dtype=F32) / HALF_ROPE)
    ang = pos.astype(F32)[:, None] * inv[None, :]
    return jnp.cos(ang), jnp.sin(ang)


def _head_block_tables(pos):
    cos, sin = _rope_angles(pos)
    n = pos.shape[0]
    cos_tab = jnp.concatenate([jnp.ones((n, NOPE_DIM), F32), cos, cos,
                               jnp.ones((n, LANES - QK_DIM), F32)], axis=1)
    sin_tab = jnp.concatenate([jnp.zeros((n, NOPE_DIM), F32), -sin, sin,
                               jnp.zeros((n, LANES - QK_DIM), F32)], axis=1)
    return cos_tab, sin_tab


def kernel(x_prompt, x_sample, state_conv_a, state_conv_b, state_ssm, cache_latent, cache_krope, page_table,
           norm_w, w_in, conv_a_w, conv_a_b, ln_a_w, ln_a_b, w_a_out, conv_b_w, conv_b_b, dt_bias, a_log,
           d_skip, norm_b_w, w_b_out, q_a_norm_w, w_uq, kv_a_norm_w, w_uk, w_uv, q_norm_w, k_norm_w,
           w_c_out, w_out):
    depth = w_in.shape[0]
    b_p, t_p, _ = x_prompt.shape
    b_s, t_s, _ = x_sample.shape
    n_pages = page_table.shape[1]
    past = n_pages * PAGE
    tm = 256
    assert (b_s * t_s) % tm == 0 and tm % t_s == 0 and t_p % tm == 0

    row3 = lambda w: w[:, None, :]
    others = dict(
        norm_w=row3(norm_w), conv_a_w=conv_a_w, conv_a_b=row3(conv_a_b), ln_a_w=row3(ln_a_w),
        ln_a_b=row3(ln_a_b), conv_b_w=conv_b_w, conv_b_b=row3(conv_b_b), norm_b_w=row3(norm_b_w),
        q_a_norm_w=row3(q_a_norm_w), kv_a_norm_w=row3(kv_a_norm_w),
        w_a_out=w_a_out.astype(BF16), w_b_out=w_b_out.astype(BF16), w_c_out=w_c_out.astype(BF16),
        w_out=w_out.astype(BF16))
    wp = _prepare_weights(w_in, dt_bias, a_log, d_skip, w_uq, w_uk, w_uv, q_norm_w, k_norm_w, others)

    cos_p, sin_p = _head_block_tables(jnp.arange(t_p, dtype=jnp.int32))
    pos_s = past + jnp.arange(t_s, dtype=jnp.int32)
    cos_s, sin_s = _head_block_tables(jnp.tile(pos_s, tm // t_s))
    cos_k, sin_k = (tab.T for tab in _rope_angles(jnp.arange(past + PAGE, dtype=jnp.int32)))
    cache_krope_t = jnp.swapaxes(cache_krope, 2, 3)

    zero_a = jnp.zeros((b_p, HIST_A, D_A), F32)
    zero_b = jnp.zeros((b_p, HIST_B, CONV_B_DIM), F32)
    zero_h = jnp.zeros((b_p, SSM_HEADS, SSM_HEAD_DIM, SSM_STATE), F32)

    y_p = x_prompt.reshape(b_p * t_p, D_MODEL)
    y_s = x_sample.reshape(b_s * t_s, D_MODEL)
    outs_p = ([], [], [], [], [])
    outs_s = ([], [], [], [], [])
    for l in range(depth):
        (glu, sa, sz, xbc, dtv, q, lat, kr, sc, g, k, v) = _in_proj(y_p, wp, l, cos_p, sin_p, False, tm)
        ya, new_a = _conv_a(glu, sa, zero_a, wp, l, b_p, 256)
        yb, new_b, h_fin = _ssd_prompt(xbc, dtv, sz, zero_b, zero_h, wp, l, b_p)
        att = _attn_prompt(q, k, v, b_p, t_p)
        y_p = _out_proj(y_p, ya, yb, att, sc, g, wp, l, tm)
        for lst, val in zip(outs_p, (new_a, new_b, h_fin, lat.reshape(b_p, t_p, KV_LORA),
                                     kr.reshape(b_p, t_p, ROPE_DIM))):
            lst.append(val)

        (glu, sa, sz, xbc, dtv, q, lat, kr, sc, g, qabs) = _in_proj(y_s, wp, l, cos_s, sin_s, True, tm)
        ya, new_a = _conv_a(glu, sa, state_conv_a[l], wp, l, b_s, t_s)
        yb, new_b, h_fin = _ssd_sample(xbc, dtv, sz, state_conv_b[l], state_ssm, wp, l, b_s, t_s)
        qc = jnp.transpose(q[:, :, NOPE_DIM:QK_DIM], (1, 0, 2)).reshape(b_s, Q_ROWS, ROPE_DIM)
        kr_t = jnp.swapaxes(kr.reshape(b_s, t_s, ROPE_DIM), 1, 2)
        ctx = _attn_sample(page_table, qabs.reshape(b_s, Q_ROWS, KV_LORA), qc,
                           lat.reshape(b_s, t_s, KV_LORA), kr_t,
                           cache_latent, cache_krope_t, cos_k, sin_k, wp, l)
        att = _uv_proj(ctx.reshape(b_s * t_s, MLA_HEADS * KV_LORA), wp, l, tm)
        y_s = _out_proj(y_s, ya, yb, att, sc, g, wp, l, tm)
        for lst, val in zip(outs_s, (new_a, new_b, h_fin, lat.reshape(b_s, t_s, KV_LORA),
                                     kr.reshape(b_s, t_s, ROPE_DIM))):
            lst.append(val)

    stack = lambda lists: [jnp.stack(v, axis=0) for v in lists]
    return (y_p.reshape(b_p, t_p, D_MODEL), y_s.reshape(b_s, t_s, D_MODEL), *stack(outs_p), *stack(outs_s))
```

```python
import functools

import jax
import jax.numpy as jnp
from jax import lax
from jax.experimental import pallas as pl
from jax.experimental.pallas import tpu as pltpu

F32 = jnp.float32
BF16 = jnp.bfloat16
HIGHEST = lax.Precision.HIGHEST

D_MODEL = 1024
D_A = 512
CONV_A_WIDTH = 31
D_B = 1024
SSM_HEAD_DIM = 64
SSM_HEADS = 16
SSM_GROUPS = 2
SSM_STATE = 128
CONV_B_WIDTH = 4
CONV_B_DIM = D_B + 2 * SSM_GROUPS * SSM_STATE
MLA_HEADS = 8
NOPE_DIM = 64
ROPE_DIM = 32
HALF_ROPE = ROPE_DIM // 2
QK_DIM = NOPE_DIM + ROPE_DIM
V_DIM = 64
D_C = MLA_HEADS * V_DIM
Q_LORA = 384
KV_LORA = 256
ROPE_BASE = 10000.0
EPS = 1e-6
N_BRANCH = 3
PAGE = 128

LANES = 128
SUBLANES = 8
HEAD_BLOCK = LANES
QK_PAD = MLA_HEADS * HEAD_BLOCK
W1_COLS = 3 * D_A + D_B + CONV_B_DIM
OFF_CQ = 0
OFF_CKV = OFF_CQ + Q_LORA
OFF_CS = OFF_CKV + KV_LORA
OFF_MG = OFF_CS + D_C
OFF_KR = OFF_MG + N_BRANCH * D_MODEL
OFF_DT = OFF_KR + LANES
W2_COLS = OFF_DT + LANES

VMEM_LIMIT = 56 * 1024 * 1024


def _dot(a, b):
    return jnp.dot(a.astype(BF16), b.astype(BF16), preferred_element_type=F32)


def _dot_nt(a, b):
    return lax.dot_general(a.astype(BF16), b.astype(BF16), (((1,), (1,)), ((), ())),
                           preferred_element_type=F32)


def _sigmoid(x):
    return 1.0 / (1.0 + jnp.exp(-x))


def _silu(x):
    return x * _sigmoid(x)


def _rms(x, w):
    return x * lax.rsqrt(jnp.mean(x * x, axis=-1, keepdims=True) + EPS) * w


def _expand_heads(x, rows):
    lane = lax.broadcasted_iota(jnp.int32, (rows, LANES), 1)
    blocks = []
    for i in range(SSM_HEADS // 2):
        blocks.append(jnp.where(lane < SSM_HEAD_DIM, x[:, 2 * i:2 * i + 1], x[:, 2 * i + 1:2 * i + 2]))
    return jnp.concatenate(blocks, axis=1)


def _const_spec(shape, layer=None):
    if layer is None:
        return pl.BlockSpec(shape, lambda *_: (0,) * len(shape), pipeline_mode=pl.Buffered(1))
    return pl.BlockSpec((None,) + shape, lambda *_: (layer,) + (0,) * len(shape),
                        pipeline_mode=pl.Buffered(1))


def _params(n_axes):
    return pltpu.CompilerParams(dimension_semantics=("arbitrary",) * n_axes,
                                vmem_limit_bytes=VMEM_LIMIT)


def _norm_rope_block(val, w_row, cosb, sinb, lane):
    ss = jnp.sum(val * val, axis=-1, keepdims=True)
    vn = val * lax.rsqrt(ss * (1.0 / QK_DIM) + EPS) * w_row
    sw = jnp.where(lane < NOPE_DIM + HALF_ROPE,
                   pltpu.roll(vn, LANES - HALF_ROPE, 1), pltpu.roll(vn, HALF_ROPE, 1))
    return vn * cosb + sw * sinb


def _in_kernel(with_abs, x_ref, nw_ref, w1_ref, w2_ref, dtb_ref, qaw_ref, wuq_ref, kvw_ref, wuk_ref,
               wuv_ref, qnw_ref, knw_ref, cos_ref, sin_ref, *rest):
    if with_abs:
        knw_nope_ref, wabs_ref = rest[:2]
        rest = rest[2:]
    glu_ref, sa_ref, sz_ref, xbc_ref, dtv_ref, q_ref, lat_ref, kr_ref, sc_ref, g_ref = rest[:10]
    if with_abs:
        qabs_ref = rest[10]
    else:
        k_ref, v_ref = rest[10:12]
    tm = x_ref.shape[0]
    x = x_ref[...]
    hb = _rms(x, nw_ref[...]).astype(BF16)

    a_val = _dot(hb, w1_ref[:, 0:D_A])
    a_gate = _dot(hb, w1_ref[:, D_A:2 * D_A])
    glu_ref[...] = a_val * _sigmoid(a_gate)
    sa_ref[...] = _silu(_dot(hb, w1_ref[:, 2 * D_A:3 * D_A]))
    sz_ref[...] = _silu(_dot(hb, w1_ref[:, 3 * D_A:3 * D_A + D_B]))
    xbc_ref[...] = _dot(hb, w1_ref[:, 3 * D_A + D_B:W1_COLS])

    cq = _dot(hb, w2_ref[:, OFF_CQ:OFF_CKV])
    ckv = _dot(hb, w2_ref[:, OFF_CKV:OFF_CS])
    sc_ref[...] = _silu(_dot(hb, w2_ref[:, OFF_CS:OFF_MG]))
    g_ref[...] = _sigmoid(_dot(hb, w2_ref[:, OFF_MG:OFF_KR]))
    krp = _dot(hb, w2_ref[:, OFF_KR:OFF_DT])
    dtp = _dot(hb, w2_ref[:, OFF_DT:W2_COLS])
    kr_ref[...] = krp[:, NOPE_DIM:QK_DIM]
    dtx = dtp + dtb_ref[...]
    dtv_ref[...] = jnp.maximum(dtx, 0.0) + jnp.log1p(jnp.exp(-jnp.abs(dtx)))

    q = _dot(_rms(cq, qaw_ref[...]), wuq_ref[...])
    latn = _rms(ckv, kvw_ref[...])
    lat_ref[...] = latn
    latb = latn.astype(BF16)
    if not with_abs:
        kn = _dot(latb, wuk_ref[...])
        for p in range(MLA_HEADS // 2):
            v_ref[p] = _dot(latb, wuv_ref[:, p * LANES:(p + 1) * LANES]).astype(BF16)

    cosb = cos_ref[...]
    sinb = sin_ref[...]
    lane = lax.broadcasted_iota(jnp.int32, (tm, LANES), 1)
    scale = QK_DIM ** -0.5
    for h in range(MLA_HEADS):
        sl = slice(h * HEAD_BLOCK, (h + 1) * HEAD_BLOCK)
        qr = _norm_rope_block(q[:, sl], qnw_ref[...], cosb, sinb, lane) * scale
        q_ref[h] = qr.astype(BF16)
        if with_abs:
            qa = _dot(qr * knw_nope_ref[...], wabs_ref[sl, :])
            qabs_ref[:, h * KV_LORA:(h + 1) * KV_LORA] = qa.astype(BF16)
        else:
            kr = _norm_rope_block(kn[:, sl] + krp, knw_ref[...], cosb, sinb, lane)
            k_ref[h] = kr.astype(BF16)


def _in_proj(x, wp, layer, cos_tab, sin_tab, with_abs, tm=256):
    n = x.shape[0]
    assert n % tm == 0 and cos_tab.shape[0] % tm == 0
    nper = cos_tab.shape[0] // tm
    row = lambda w: pl.BlockSpec((tm, w), lambda i: (i, 0))
    tab = pl.BlockSpec((tm, LANES), lambda i: (i % nper, 0))
    in_specs = [
        row(D_MODEL),
        _const_spec((1, D_MODEL), layer),
        _const_spec((D_MODEL, W1_COLS), layer),
        _const_spec((D_MODEL, W2_COLS), layer),
        _const_spec((1, LANES), layer),
        _const_spec((1, Q_LORA), layer),
        _const_spec((Q_LORA, QK_PAD), layer),
        _const_spec((1, KV_LORA), layer),
        _const_spec((KV_LORA, QK_PAD), layer),
        _const_spec((KV_LORA, D_C), layer),
        _const_spec((1, LANES), layer),
        _const_spec((1, LANES), layer),
        tab, tab,
    ]
    args = [x, wp['norm_w'], wp['w1'], wp['w2'], wp['dt_bias'], wp['q_a_norm_w'], wp['w_uq'],
            wp['kv_a_norm_w'], wp['w_uk'], wp['w_uv'], wp['q_norm_w'], wp['k_norm_w'], cos_tab, sin_tab]
    heads = lambda nh: (jax.ShapeDtypeStruct((nh, n, LANES), BF16),
                        pl.BlockSpec((nh, tm, LANES), lambda i: (0, i, 0)))
    flat = lambda w, dt: (jax.ShapeDtypeStruct((n, w), dt), row(w))
    outs = [flat(D_A, F32), flat(D_A, F32), flat(D_B, F32), flat(CONV_B_DIM, F32), flat(LANES, F32),
            heads(MLA_HEADS), flat(KV_LORA, F32), flat(ROPE_DIM, F32), flat(D_C, F32),
            flat(N_BRANCH * D_MODEL, F32)]
    if with_abs:
        in_specs += [_const_spec((1, LANES), layer), _const_spec((QK_PAD, KV_LORA), layer)]
        args += [wp['k_norm_w_nope'], wp['w_uk_abs']]
        outs.append(flat(MLA_HEADS * KV_LORA, BF16))
    else:
        outs += [heads(MLA_HEADS), heads(MLA_HEADS // 2)]
    return pl.pallas_call(
        functools.partial(_in_kernel, with_abs),
        out_shape=[o[0] for o in outs],
        grid=(n // tm,),
        in_specs=in_specs,
        out_specs=[o[1] for o in outs],
        compiler_params=_params(1),
    )(*args)


HIST_A = CONV_A_WIDTH - 1
HIST_A_PAD = 32


def _conv_a_kernel(glu_ref, sa_ref, hist_ref, *refs):
    seqs = hist_ref.shape[0]
    tt = glu_ref.shape[0] // seqs
    for s in range(seqs):
        _conv_a_one(s, tt, glu_ref, sa_ref, hist_ref, *refs)


def _conv_a_one(s, tt, glu_ref, sa_ref, hist_ref, cw_ref, cb_ref, lw_ref, lb_ref, ya_ref, newa_ref, bufs):
    t = pl.program_id(1)
    lo = HIST_A_PAD - HIST_A
    rows = slice(s * tt, (s + 1) * tt)
    buf = bufs.at[s]

    @pl.when(t == 0)
    def _():
        buf[lo:HIST_A_PAD, :] = hist_ref[s]
        buf[HIST_A_PAD + tt:HIST_A_PAD + tt + SUBLANES, :] = jnp.zeros((SUBLANES, D_A), F32)

    buf[HIST_A_PAD:HIST_A_PAD + tt, :] = glu_ref[rows, :]
    acc = jnp.broadcast_to(cb_ref[...], (tt, D_A))
    for r in range(SUBLANES):
        part = None
        for a in range((lo + CONV_A_WIDTH - 1) // SUBLANES + 1):
            k = SUBLANES * a + r - lo
            if 0 <= k < CONV_A_WIDTH:
                term = cw_ref[k:k + 1, :] * buf[SUBLANES * a:SUBLANES * a + tt + SUBLANES, :]
                part = term if part is None else part + term
        acc = acc + part[r:r + tt, :]
    xc = acc - jnp.mean(acc, axis=-1, keepdims=True)
    ln = xc * lax.rsqrt(jnp.mean(xc * xc, axis=-1, keepdims=True) + EPS) * lw_ref[...] + lb_ref[...]
    ya_ref[rows, :] = _silu(ln) * sa_ref[rows, :]
    tail = buf[tt + lo:tt + HIST_A_PAD, :]

    @pl.when(t == pl.num_programs(1) - 1)
    def _():
        newa_ref[s] = tail

    buf[lo:HIST_A_PAD, :] = tail


def _conv_a(glu, sa, hist, wp, layer, batch, tt, seqs=1):
    n = glu.shape[0]
    nt = n // (batch * tt)
    assert batch % seqs == 0 and (seqs == 1 or nt == 1)
    row = pl.BlockSpec((seqs * tt, D_A), lambda b, t: (b * nt + t, 0))
    histb = pl.BlockSpec((seqs, HIST_A, D_A), lambda b, t: (b, 0, 0))
    return pl.pallas_call(
        _conv_a_kernel,
        out_shape=[jax.ShapeDtypeStruct((n, D_A), F32),
                   jax.ShapeDtypeStruct((batch, HIST_A, D_A), F32)],
        grid=(batch // seqs, nt),
        in_specs=[row, row, histb,
                  _const_spec((CONV_A_WIDTH, D_A), layer),
                  _const_spec((1, D_A), layer), _const_spec((1, D_A), layer),
                  _const_spec((1, D_A), layer)],
        out_specs=[row, histb],
        scratch_shapes=[pltpu.VMEM((seqs, HIST_A_PAD + tt + SUBLANES, D_A), F32)],
        compiler_params=_params(2),
    )(glu, sa, hist, wp['conv_a_w'], wp['conv_a_b'], wp['ln_a_w'], wp['ln_a_b'])


HIST_B = CONV_B_WIDTH - 1
HIST_B_PAD = 8
GROUP_COLS = D_B // SSM_GROUPS
HEADS_PER_GROUP = SSM_HEADS // SSM_GROUPS


def _conv_b(cbuf, xbc_new, hist_loader, cw_ref, cb_ref, q, first):
    lo = HIST_B_PAD - HIST_B

    if first is None:
        cbuf[lo:HIST_B_PAD, :] = hist_loader()
    else:
        @pl.when(first)
        def _():
            cbuf[lo:HIST_B_PAD, :] = hist_loader()

    cbuf[HIST_B_PAD:HIST_B_PAD + q, :] = xbc_new
    acc = jnp.broadcast_to(cb_ref[...], (q, CONV_B_DIM))
    for k in range(CONV_B_WIDTH):
        acc = acc + cw_ref[k:k + 1, :] * cbuf[lo + k:lo + k + q, :]
    tail = cbuf[q + lo:q + HIST_B_PAD, :]
    return _silu(acc), tail


def _ssd_p_kernel(xbc_ref, hist_ref, cw_ref, cb_ref, dtv_ref, a_ref, dsk_ref, sz_ref, nbw_ref, h0_ref,
                  yb_ref, newb_ref, hfin_ref, cbuf, ht):
    c = pl.program_id(1)
    last = c == pl.num_programs(1) - 1
    q = xbc_ref.shape[0]
    xbc, tail = _conv_b(cbuf, xbc_ref[...], lambda: hist_ref[0], cw_ref, cb_ref, q, c == 0)

    @pl.when(c == 0)
    def _():
        ht[...] = h0_ref[0].reshape(SSM_HEADS * SSM_HEAD_DIM, SSM_STATE).T

    @pl.when(last)
    def _():
        newb_ref[0] = tail

    cbuf[HIST_B_PAD - HIST_B:HIST_B_PAD, :] = tail

    xs = xbc[:, :D_B]
    bs = xbc[:, D_B:D_B + SSM_GROUPS * SSM_STATE]
    cs = xbc[:, D_B + SSM_GROUPS * SSM_STATE:]
    dt = dtv_ref[...]
    adt = dt * a_ref[...]
    row = lax.broadcasted_iota(jnp.int32, (q, q), 0)
    col = lax.broadcasted_iota(jnp.int32, (q, q), 1)
    tril = row >= col
    acum = jnp.dot(tril.astype(F32), adt, precision=HIGHEST, preferred_element_type=F32)
    acum_t = acum.T
    dt_t = dt.T
    alast = acum[q - 1:q, :]
    dec_end = jnp.exp(alast - acum) * dt
    xsb = xs.astype(BF16)
    xw = (xs * _expand_heads(dec_end, q)).astype(BF16)
    chunk_decay = _expand_heads(jnp.exp(alast), 1)
    lane4 = lax.broadcasted_iota(jnp.int32, (q, 4 * SSM_HEAD_DIM), 1) // SSM_HEAD_DIM

    y_diag = []
    y_off = []
    for g in range(SSM_GROUPS):
        bsg = bs[:, g * SSM_STATE:(g + 1) * SSM_STATE]
        csg = cs[:, g * SSM_STATE:(g + 1) * SSM_STATE]
        cb = _dot_nt(csg, bsg)
        for quad in range(HEADS_PER_GROUP // 4):
            ws = []
            rhs = []
            qd = g * (HEADS_PER_GROUP // 4) + quad
            xblk = xsb[:, qd * 4 * SSM_HEAD_DIM:(qd + 1) * 4 * SSM_HEAD_DIM]
            for hh in range(4):
                h = qd * 4 + hh
                seg = acum[:, h:h + 1] - acum_t[h:h + 1, :]
                lm = jnp.exp(jnp.where(tril, seg, -jnp.inf))
                ws.append((cb * lm * dt_t[h:h + 1, :]).astype(BF16))
                rhs.append(jnp.where(lane4 == hh, xblk, jnp.zeros_like(xblk)))
            y_diag.append(jnp.dot(jnp.concatenate(ws, axis=1), jnp.concatenate(rhs, axis=0),
                                  preferred_element_type=F32))
        gs = slice(g * GROUP_COLS, (g + 1) * GROUP_COLS)
        htg = ht[:, gs]
        y_off.append(_dot(csg, htg))
        ht[:, gs] = htg * chunk_decay[:, gs] + _dot(bsg.T, xw[:, gs])

    y = (jnp.concatenate(y_diag, axis=1)
         + jnp.concatenate(y_off, axis=1) * _expand_heads(jnp.exp(acum), q)
         + dsk_ref[...] * xs)
    yb_ref[...] = _rms(y * sz_ref[...], nbw_ref[...])

    @pl.when(last)
    def _():
        hfin_ref[0] = ht[...].T.reshape(SSM_HEADS, SSM_HEAD_DIM, SSM_STATE)


def _ssd_prompt(xbc, dtv, sz, hist, h0, wp, layer, batch, q=128):
    n = xbc.shape[0]
    nc = n // (batch * q)
    row = lambda w: pl.BlockSpec((q, w), lambda b, c: (b * nc + c, 0))
    state = pl.BlockSpec((1, SSM_HEADS, SSM_HEAD_DIM, SSM_STATE), lambda b, c: (b, 0, 0, 0))
    histb = pl.BlockSpec((1, HIST_B, CONV_B_DIM), lambda b, c: (b, 0, 0))
    return pl.pallas_call(
        _ssd_p_kernel,
        out_shape=[jax.ShapeDtypeStruct((n, D_B), F32),
                   jax.ShapeDtypeStruct((batch, HIST_B, CONV_B_DIM), F32),
                   jax.ShapeDtypeStruct((batch, SSM_HEADS, SSM_HEAD_DIM, SSM_STATE), F32)],
        grid=(batch, nc),
        in_specs=[row(CONV_B_DIM), histb,
                  _const_spec((CONV_B_WIDTH, CONV_B_DIM), layer), _const_spec((1, CONV_B_DIM), layer),
                  row(LANES), _const_spec((1, LANES), layer), _const_spec((1, D_B), layer),
                  row(D_B), _const_spec((1, D_B), layer), state],
        out_specs=[row(D_B), histb, state],
        scratch_shapes=[pltpu.VMEM((HIST_B_PAD + q, CONV_B_DIM), F32),
                        pltpu.VMEM((SSM_STATE, D_B), F32)],
        compiler_params=_params(2),
    )(xbc, hist, wp['conv_b_w'], wp['conv_b_b'], dtv, wp['a_neg'], wp['d_skip'], sz, wp['norm_b_w'], h0)


def _ssd_s_kernel(q, xbc_ref, *refs):
    for s in range(xbc_ref.shape[0] // q):
        _ssd_s_one(s, q, xbc_ref, *refs)


def _ssd_s_one(s, q, xbc_ref, hist_ref, cw_ref, cb_ref, dtv_ref, a_ref, dsk_ref, sz_ref, nbw_ref, h0_ref,
               yb_ref, newb_ref, hfin_ref, cbuf):
    assert q * SSM_HEADS == LANES
    rows = slice(s * q, (s + 1) * q)
    xbc, tail = _conv_b(cbuf.at[s], xbc_ref[rows, :], lambda: hist_ref[s], cw_ref, cb_ref, q, None)
    newb_ref[s] = tail
    xs = xbc[:, :D_B]
    bs = xbc[:, D_B:D_B + SSM_GROUPS * SSM_STATE]
    cs = xbc[:, D_B + SSM_GROUPS * SSM_STATE:]
    dt = dtv_ref[rows, :]
    adt = dt * a_ref[...]
    row = lax.broadcasted_iota(jnp.int32, (q, LANES), 0)
    lane = lax.broadcasted_iota(jnp.int32, (q, LANES), 1)
    tril8 = (row >= lane).astype(F32)[:, :q]
    acum = jnp.dot(tril8, adt, precision=HIGHEST, preferred_element_type=F32)
    er = lax.broadcasted_iota(jnp.int32, (LANES, LANES), 0)
    ec = lax.broadcasted_iota(jnp.int32, (LANES, LANES), 1)
    rep = (er == ec // q).astype(F32)
    a_col = jnp.dot(acum, rep, precision=HIGHEST, preferred_element_type=F32)
    d_col = jnp.dot(dt, rep, precision=HIGHEST, preferred_element_type=F32)
    sel = row == lane % q
    a_row = jnp.sum(jnp.where(sel, a_col, 0.0), axis=0, keepdims=True)
    d_row = jnp.sum(jnp.where(sel, d_col, 0.0), axis=0, keepdims=True)
    lm = jnp.exp(jnp.where(row >= lane % q, a_col - a_row, -jnp.inf))
    br = lax.broadcasted_iota(jnp.int32, (LANES, SSM_GROUPS * SSM_STATE), 0)
    bc = lax.broadcasted_iota(jnp.int32, (LANES, SSM_GROUPS * SSM_STATE), 1)
    bs_t = jnp.where(br // (q * HEADS_PER_GROUP) == bc // SSM_STATE,
                     jnp.concatenate([bs] * SSM_HEADS, axis=0), 0.0)
    cb_flat = _dot_nt(cs, bs_t)
    w = (cb_flat * lm * d_row).astype(BF16)
    xr = lax.broadcasted_iota(jnp.int32, (LANES, D_B), 0)
    xc = lax.broadcasted_iota(jnp.int32, (LANES, D_B), 1)
    rhs = jnp.where(xr // q == xc // SSM_HEAD_DIM, jnp.concatenate([xs] * SSM_HEADS, axis=0), 0.0)
    y_diag = jnp.dot(w, rhs.astype(BF16), preferred_element_type=F32)

    h0 = h0_ref[s].reshape(SSM_HEADS * SSM_HEAD_DIM, SSM_STATE)
    alast = acum[q - 1:q, :]
    ea_last = jnp.exp(alast)
    dec_end = jnp.exp(alast - acum) * dt
    xw = (xs * _expand_heads(dec_end, q)).astype(BF16)
    ir = lax.broadcasted_iota(jnp.int32, (GROUP_COLS, GROUP_COLS), 0)
    ic = lax.broadcasted_iota(jnp.int32, (GROUP_COLS, GROUP_COLS), 1)
    eye = (ir == ic).astype(BF16)
    y_off = []
    for g in range(SSM_GROUPS):
        gs = slice(g * GROUP_COLS, (g + 1) * GROUP_COLS)
        bsg = bs[:, g * SSM_STATE:(g + 1) * SSM_STATE]
        csg = cs[:, g * SSM_STATE:(g + 1) * SSM_STATE]
        y_off.append(_dot_nt(csg, h0[gs, :]))
        xw_t = _dot_nt(eye, xw[:, gs])
        st = _dot(xw_t, bsg)
        for hh in range(HEADS_PER_GROUP):
            h = g * HEADS_PER_GROUP + hh
            rs = slice(hh * SSM_HEAD_DIM, (hh + 1) * SSM_HEAD_DIM)
            hfin_ref[s, h] = h0_ref[s, h] * ea_last[:, h:h + 1] + st[rs, :]
    y = (y_diag + jnp.concatenate(y_off, axis=1) * _expand_heads(jnp.exp(acum), q)
         + dsk_ref[...] * xs)
    yb_ref[rows, :] = _rms(y * sz_ref[rows, :], nbw_ref[...])


def _ssd_sample(xbc, dtv, sz, hist, h0, wp, layer, batch, q, seqs=4):
    n = xbc.shape[0]
    assert batch % seqs == 0
    row = lambda w: pl.BlockSpec((seqs * q, w), lambda b: (b, 0))
    state = pl.BlockSpec((seqs, SSM_HEADS, SSM_HEAD_DIM, SSM_STATE), lambda b: (b, 0, 0, 0))
    state_in = pl.BlockSpec((None, seqs, SSM_HEADS, SSM_HEAD_DIM, SSM_STATE),
                            lambda b: (layer, b, 0, 0, 0))
    histb = pl.BlockSpec((seqs, HIST_B, CONV_B_DIM), lambda b: (b, 0, 0))
    return pl.pallas_call(
        functools.partial(_ssd_s_kernel, q),
        out_shape=[jax.ShapeDtypeStruct((n, D_B), F32),
                   jax.ShapeDtypeStruct((batch, HIST_B, CONV_B_DIM), F32),
                   jax.ShapeDtypeStruct((batch, SSM_HEADS, SSM_HEAD_DIM, SSM_STATE), F32)],
        grid=(batch // seqs,),
        in_specs=[row(CONV_B_DIM), histb,
                  _const_spec((CONV_B_WIDTH, CONV_B_DIM), layer), _const_spec((1, CONV_B_DIM), layer),
                  row(LANES), _const_spec((1, LANES), layer), _const_spec((1, D_B), layer),
                  row(D_B), _const_spec((1, D_B), layer), state_in],
        out_specs=[row(D_B), histb, state],
        scratch_shapes=[pltpu.VMEM((seqs, HIST_B_PAD + q, CONV_B_DIM), F32)],
        compiler_params=_params(1),
    )(xbc, hist, wp['conv_b_w'], wp['conv_b_b'], dtv, wp['a_neg'], wp['d_skip'], sz, wp['norm_b_w'], h0)


def _attn_p_kernel(tq, q_ref, k_ref, v_ref, o_ref):
    seq = q_ref.shape[1]
    row = lax.broadcasted_iota(jnp.int32, (tq, tq), 0)
    col = lax.broadcasted_iota(jnp.int32, (tq, tq), 1)
    causal = row >= col
    lane = lax.broadcasted_iota(jnp.int32, (tq, LANES), 1)
    for i in range(seq // tq):
        lo = i * tq
        rows = slice(lo, lo + tq)

        def pair(p, carry, lo=lo, rows=rows):
            outs = []
            for e in range(2):
                h = 2 * p + e
                qh = q_ref[h, rows, :]
                s_d = jnp.where(causal, _dot_nt(qh, k_ref[h, rows, :]), -jnp.inf)
                m = jnp.max(s_d, axis=-1, keepdims=True)
                if lo:
                    s_p = _dot_nt(qh, k_ref[h, 0:lo, :])
                    m = jnp.maximum(m, jnp.max(s_p, axis=-1, keepdims=True))
                p_d = jnp.exp(s_d - m)
                l = jnp.sum(p_d, axis=-1, keepdims=True)
                pv = jnp.dot(p_d.astype(BF16), v_ref[p, rows, :], preferred_element_type=F32)
                if lo:
                    p_p = jnp.exp(s_p - m)
                    l = l + jnp.sum(p_p, axis=-1, keepdims=True)
                    pv = pv + jnp.dot(p_p.astype(BF16), v_ref[p, 0:lo, :], preferred_element_type=F32)
                outs.append(pv * (1.0 / l))
            o_ref[p, rows, :] = jnp.where(lane < V_DIM, outs[0], outs[1])
            return carry

        lax.fori_loop(0, MLA_HEADS // 2, pair, 0)


def _attn_prompt(q, k, v, batch, seq, tq=256):
    n = q.shape[1]
    head_rows = lambda nh: pl.BlockSpec((nh, seq, LANES), lambda b: (0, b, 0))
    return pl.pallas_call(
        functools.partial(_attn_p_kernel, tq),
        out_shape=jax.ShapeDtypeStruct((MLA_HEADS // 2, n, LANES), F32),
        grid=(batch,),
        in_specs=[head_rows(MLA_HEADS), head_rows(MLA_HEADS), head_rows(MLA_HEADS // 2)],
        out_specs=head_rows(MLA_HEADS // 2),
        compiler_params=_params(1),
    )(q, k, v)


PAGES_PER_CHUNK = 16
CHUNK = PAGES_PER_CHUNK * PAGE
Q_ROWS = 64
KNT_ROWS = MLA_HEADS * NOPE_DIM
LHS_ROWS = KNT_ROWS + Q_ROWS


def _attn_s_kernel(layer, n_chunks, past, pt_ref, qn_ref, qc_ref, wukt_ref, knwr_ref, cos_ref, sin_ref,
                   latn_ref, krn_ref, clat_hbm, ckr_hbm, ctx_ref,
                   latbuf, krbuf, sems, lhs_scr, latb_scr, s_scr, newlat, newkr):
    b = pl.program_id(0)
    nb = pl.num_programs(0)
    t_new = latn_ref.shape[1]

    def copies(bb, cc):
        slot = cc % 2
        out = []
        for k in range(PAGES_PER_CHUNK):
            page = pt_ref[bb, cc * PAGES_PER_CHUNK + k]
            keys = pl.ds(k * PAGE, PAGE)
            out.append(pltpu.make_async_copy(clat_hbm.at[layer, page], latbuf.at[slot, keys],
                                             sems.at[0, slot]))
            out.append(pltpu.make_async_copy(ckr_hbm.at[layer, page], krbuf.at[slot, :, keys],
                                             sems.at[1, slot]))
        return out

    def scores(bb, latb, krt, pos0):
        ck = latb.shape[0]
        lhs_scr[KNT_ROWS:LHS_ROWS, :] = qn_ref[bb]
        big = _dot_nt(lhs_scr[...], latb)
        knt = big[0:KNT_ROWS, :]
        ssq = jnp.sum((knt * knt).reshape(MLA_HEADS, NOPE_DIM, ck), axis=1)
        ssr = jnp.sum(krt * krt, axis=0, keepdims=True)
        r = lax.rsqrt((ssq + ssr) * (1.0 / QK_DIM) + EPS)
        x = krt * knwr_ref[...]
        x1 = x[0:HALF_ROPE, :]
        x2 = x[HALF_ROPE:ROPE_DIM, :]
        cos = cos_ref[:, pos0:pos0 + ck]
        sin = sin_ref[:, pos0:pos0 + ck]
        kro = jnp.concatenate([x1 * cos - x2 * sin, x2 * cos + x1 * sin], axis=0)
        s = big[KNT_ROWS:LHS_ROWS, :] + _dot(qc_ref[bb], kro)
        return s * jnp.concatenate([r] * (Q_ROWS // MLA_HEADS), axis=0)

    def stage1(bb, slot, pos0):
        latb = latbuf[slot].astype(BF16)
        latb_scr[slot] = latb
        s_scr[slot] = scores(bb, latb, krbuf[slot], pos0)

    def stage2(s, latb, carry, mask):
        m, l, acc = carry
        if mask is not None:
            s = jnp.where(mask, s, -1e30)
        m_new = jnp.maximum(m, jnp.max(s, axis=-1, keepdims=True))
        alpha = jnp.exp(m - m_new)
        pr = jnp.exp(s - m_new)
        if mask is not None:
            pr = jnp.where(mask, pr, 0.0)
        l = alpha * l + jnp.sum(pr, axis=-1, keepdims=True)
        acc = alpha * acc + jnp.dot(pr.astype(BF16), latb, preferred_element_type=F32)
        return m_new, l, acc

    @pl.when(b == 0)
    def _():
        lhs_scr[0:KNT_ROWS, :] = wukt_ref[...]
        for c0 in range(2):
            for cp in copies(0, c0):
                cp.start()
        for cp in copies(0, 0):
            cp.wait()
        stage1(0, 0, 0)
        newlat[...] = jnp.zeros_like(newlat)
        newkr[...] = jnp.zeros_like(newkr)

    nxt = jnp.minimum(b + 1, nb - 1)
    carry = (jnp.full((Q_ROWS, 1), -1e30, F32), jnp.zeros((Q_ROWS, 1), F32),
             jnp.zeros((Q_ROWS, KV_LORA), F32))
    def issue(c, ahead, wait):
        wraps = c + ahead >= n_chunks

        def run():
            for cp in copies(b + 1 if wraps else b, (c + ahead) % n_chunks):
                cp.wait() if wait else cp.start()

        if wraps:
            pl.when(b + 1 < nb)(run)
        else:
            run()

    for c in range(n_chunks):
        issue(c, 1, True)
        issue(c, 2, False)
        wraps = c + 1 >= n_chunks
        stage1(nxt if wraps else b, (c + 1) % 2, ((c + 1) % n_chunks) * CHUNK)
        carry = stage2(s_scr[c % 2], latb_scr[c % 2], carry, None)

    newlat[0:t_new, :] = latn_ref[b]
    newkr[:, 0:t_new] = krn_ref[b]
    qi = lax.broadcasted_iota(jnp.int32, (Q_ROWS, PAGE), 0) // MLA_HEADS
    kj = lax.broadcasted_iota(jnp.int32, (Q_ROWS, PAGE), 1)
    latb = newlat[...].astype(BF16)
    _, l, acc = stage2(scores(b, latb, newkr[...], past), latb, carry, kj <= qi)
    ctx_ref[0] = acc * (1.0 / l)


def _attn_sample(page_table, qn, qc, latn, krn_t, cache_latent, cache_krope_t, cos_t, sin_t, wp, layer):
    batch, n_pages = page_table.shape
    n_chunks = n_pages // PAGES_PER_CHUNK
    assert n_pages % PAGES_PER_CHUNK == 0 and n_chunks % 2 == 0
    past = n_pages * PAGE
    per_seq = lambda shape: pl.BlockSpec((1,) + shape, lambda b, pt: (b, 0, 0))
    grid_spec = pltpu.PrefetchScalarGridSpec(
        num_scalar_prefetch=1,
        grid=(batch,),
        in_specs=[_const_spec(qn.shape), _const_spec(qc.shape),
                  _const_spec((KNT_ROWS, KV_LORA), layer),
                  _const_spec((ROPE_DIM, 1), layer),
                  _const_spec(cos_t.shape), _const_spec(sin_t.shape),
                  _const_spec(latn.shape), _const_spec(krn_t.shape),
                  pl.BlockSpec(memory_space=pl.ANY), pl.BlockSpec(memory_space=pl.ANY)],
        out_specs=per_seq((Q_ROWS, KV_LORA)),
        scratch_shapes=[pltpu.VMEM((2, CHUNK, KV_LORA), F32),
                        pltpu.VMEM((2, ROPE_DIM, CHUNK), F32),
                        pltpu.SemaphoreType.DMA((2, 2)),
                        pltpu.VMEM((LHS_ROWS, KV_LORA), BF16),
                        pltpu.VMEM((2, CHUNK, KV_LORA), BF16),
                        pltpu.VMEM((2, Q_ROWS, CHUNK), F32),
                        pltpu.VMEM((PAGE, KV_LORA), F32),
                        pltpu.VMEM((ROPE_DIM, PAGE), F32)],
    )
    return pl.pallas_call(
        functools.partial(_attn_s_kernel, layer, n_chunks, past),
        out_shape=jax.ShapeDtypeStruct((batch, Q_ROWS, KV_LORA), F32),
        grid_spec=grid_spec,
        compiler_params=_params(1),
    )(page_table, qn, qc, wp['w_uk_t'], wp['k_norm_w_rope'], cos_t, sin_t, latn, krn_t,
      cache_latent, cache_krope_t)


def _uv_kernel(ctx_ref, wv_ref, att_ref):
    for p in range(MLA_HEADS // 2):
        acc = None
        for e in range(2):
            h = 2 * p + e
            part = _dot(ctx_ref[:, h * KV_LORA:(h + 1) * KV_LORA], wv_ref[h])
            acc = part if acc is None else acc + part
        att_ref[p] = acc


def _uv_proj(ctx, wp, layer, tm=256):
    n = ctx.shape[0]
    tm = min(tm, n)
    return pl.pallas_call(
        _uv_kernel,
        out_shape=jax.ShapeDtypeStruct((MLA_HEADS // 2, n, LANES), F32),
        grid=(n // tm,),
        in_specs=[pl.BlockSpec((tm, MLA_HEADS * KV_LORA), lambda i: (i, 0)),
                  _const_spec((MLA_HEADS, KV_LORA, LANES), layer)],
        out_specs=pl.BlockSpec((MLA_HEADS // 2, tm, LANES), lambda i: (0, i, 0)),
        compiler_params=_params(1),
    )(ctx, wp['w_uv_pair'])


def _out_kernel(x_ref, ya_ref, yb_ref, att_ref, sc_ref, g_ref, wa_ref, wb_ref, wc_ref, wo_ref, y_ref):
    br_a = _dot(ya_ref[...], wa_ref[...])
    br_b = _dot(yb_ref[...], wb_ref[...])
    att = jnp.concatenate([att_ref[p] for p in range(MLA_HEADS // 2)], axis=1)
    br_c = _dot(att * sc_ref[...], wc_ref[...])
    merged = (g_ref[:, 0:D_MODEL] * br_a + g_ref[:, D_MODEL:2 * D_MODEL] * br_b
              + g_ref[:, 2 * D_MODEL:3 * D_MODEL] * br_c)
    y_ref[...] = x_ref[...] + _dot(merged, wo_ref[...])


def _out_proj(x, ya, yb, att, sc, g, wp, layer, tm=256):
    n = x.shape[0]
    row = lambda w: pl.BlockSpec((tm, w), lambda i: (i, 0))
    return pl.pallas_call(
        _out_kernel,
        out_shape=jax.ShapeDtypeStruct((n, D_MODEL), F32),
        grid=(n // tm,),
        in_specs=[row(D_MODEL), row(D_A), row(D_B),
                  pl.BlockSpec((MLA_HEADS // 2, tm, LANES), lambda i: (0, i, 0)),
                  row(D_C), row(N_BRANCH * D_MODEL),
                  _const_spec((D_A, D_MODEL), layer), _const_spec((D_B, D_MODEL), layer),
                  _const_spec((D_C, D_MODEL), layer), _const_spec((D_MODEL, D_MODEL), layer)],
        out_specs=row(D_MODEL),
        compiler_params=_params(1),
    )(x, ya, yb, att, sc, g, wp['w_a_out'], wp['w_b_out'], wp['w_c_out'], wp['w_out'])


def _pad_lanes(row, width=LANES):
    return jnp.pad(row, ((0, 0), (0, 0), (0, width - row.shape[-1])))


def _prepare_weights(w_in, dt_bias, a_log, d_skip, w_uq, w_uk, w_uv, q_norm_w, k_norm_w, others):
    depth = w_in.shape[0]
    o = 0
    cuts = {}
    for name, width in (('a', 3 * D_A), ('z', D_B), ('xbc', CONV_B_DIM), ('dt', SSM_HEADS), ('cq', Q_LORA),
                        ('ckv', KV_LORA), ('kr', ROPE_DIM), ('cs', D_C), ('mg', N_BRANCH * D_MODEL)):
        cuts[name] = (o, o + width)
        o += width
    col = lambda name: w_in[:, :, cuts[name][0]:cuts[name][1]]
    zeros = lambda width: jnp.zeros((depth, D_MODEL, width), w_in.dtype)
    w1 = w_in[:, :, 0:W1_COLS].astype(BF16)
    w2 = jnp.concatenate([col('cq'), col('ckv'), col('cs'), col('mg'),
                          zeros(NOPE_DIM), col('kr'), zeros(LANES - QK_DIM),
                          col('dt'), zeros(LANES - SSM_HEADS)], axis=-1).astype(BF16)
    head_pad = lambda w: jnp.pad(w, ((0, 0), (0, 0), (0, 0), (0, HEAD_BLOCK - w.shape[-1])))
    wuq = head_pad(w_uq.reshape(depth, Q_LORA, MLA_HEADS, QK_DIM)).reshape(depth, Q_LORA, QK_PAD)
    wuk = head_pad(w_uk).reshape(depth, KV_LORA, QK_PAD)
    wuk_abs = jnp.transpose(head_pad(w_uk), (0, 2, 3, 1)).reshape(depth, QK_PAD, KV_LORA)
    wuk_t = jnp.transpose(w_uk, (0, 2, 3, 1)).reshape(depth, MLA_HEADS * NOPE_DIM, KV_LORA)
    wuv = w_uv.reshape(depth, KV_LORA, D_C)
    wv_h = jnp.transpose(w_uv, (0, 2, 1, 3))
    left = jnp.pad(wv_h, ((0, 0), (0, 0), (0, 0), (0, V_DIM)))
    right = jnp.pad(wv_h, ((0, 0), (0, 0), (0, 0), (V_DIM, 0)))
    even = (jnp.arange(MLA_HEADS) % 2 == 0)[None, :, None, None]
    wuv_pair = jnp.where(even, left, right)
    wp = dict(others)
    wp.update(
        w1=w1, w2=w2,
        dt_bias=_pad_lanes(dt_bias[:, None, :]),
        a_neg=_pad_lanes(-jnp.exp(a_log.astype(F32))[:, None, :]),
        d_skip=jnp.repeat(d_skip, SSM_HEAD_DIM, axis=-1)[:, None, :],
        w_uq=wuq.astype(BF16), w_uk=wuk.astype(BF16), w_uv=wuv.astype(BF16),
        w_uk_abs=wuk_abs.astype(BF16), w_uk_t=wuk_t.astype(BF16), w_uv_pair=wuv_pair.astype(BF16),
        q_norm_w=_pad_lanes(q_norm_w[:, None, :]), k_norm_w=_pad_lanes(k_norm_w[:, None, :]),
        k_norm_w_nope=_pad_lanes(k_norm_w[:, None, :NOPE_DIM]),
        k_norm_w_rope=k_norm_w[:, NOPE_DIM:, None],
    )
    return wp


def _rope_angles(pos):
    inv = jnp.power(ROPE_BASE, -jnp.arange(HALF_ROPE, dtype=F32) / HALF_ROPE)
    ang = pos.astype(F32)[:, None] * inv[None, :]
    return jnp.cos(ang), jnp.sin(ang)


def _head_block_tables(pos):
    cos, sin = _rope_angles(pos)
    n = pos.shape[0]
    cos_tab = jnp.concatenate([jnp.ones((n, NOPE_DIM), F32), cos, cos,
                               jnp.ones((n, LANES - QK_DIM), F32)], axis=1)
    sin_tab = jnp.concatenate([jnp.zeros((n, NOPE_DIM), F32), -sin, sin,
                               jnp.zeros((n, LANES - QK_DIM), F32)], axis=1)
    return cos_tab, sin_tab


def kernel(x_prompt, x_sample, state_conv_a, state_conv_b, state_ssm, cache_latent, cache_krope, page_table,
           norm_w, w_in, conv_a_w, conv_a_b, ln_a_w, ln_a_b, w_a_out, conv_b_w, conv_b_b, dt_bias, a_log,
           d_skip, norm_b_w, w_b_out, q_a_norm_w, w_uq, kv_a_norm_w, w_uk, w_uv, q_norm_w, k_norm_w,
           w_c_out, w_out):
    depth = w_in.shape[0]
    b_p, t_p, _ = x_prompt.shape
    b_s, t_s, _ = x_sample.shape
    n_pages = page_table.shape[1]
    past = n_pages * PAGE
    tm = 256
    assert (b_s * t_s) % tm == 0 and tm % t_s == 0 and t_p % tm == 0

    row3 = lambda w: w[:, None, :]
    others = dict(
        norm_w=row3(norm_w), conv_a_w=conv_a_w, conv_a_b=row3(conv_a_b), ln_a_w=row3(ln_a_w),
        ln_a_b=row3(ln_a_b), conv_b_w=conv_b_w, conv_b_b=row3(conv_b_b), norm_b_w=row3(norm_b_w),
        q_a_norm_w=row3(q_a_norm_w), kv_a_norm_w=row3(kv_a_norm_w),
        w_a_out=w_a_out.astype(BF16), w_b_out=w_b_out.astype(BF16), w_c_out=w_c_out.astype(BF16),
        w_out=w_out.astype(BF16))
    wp = _prepare_weights(w_in, dt_bias, a_log, d_skip, w_uq, w_uk, w_uv, q_norm_w, k_norm_w, others)

    cos_p, sin_p = _head_block_tables(jnp.arange(t_p, dtype=jnp.int32))
    pos_s = past + jnp.arange(t_s, dtype=jnp.int32)
    cos_s, sin_s = _head_block_tables(jnp.tile(pos_s, tm // t_s))
    cos_k, sin_k = (tab.T for tab in _rope_angles(jnp.arange(past + PAGE, dtype=jnp.int32)))
    cache_krope_t = jnp.swapaxes(cache_krope, 2, 3)

    zero_a = jnp.zeros((b_p, HIST_A, D_A), F32)
    zero_b = jnp.zeros((b_p, HIST_B, CONV_B_DIM), F32)
    zero_h = jnp.zeros((b_p, SSM_HEADS, SSM_HEAD_DIM, SSM_STATE), F32)

    y_p = x_prompt.reshape(b_p * t_p, D_MODEL)
    y_s = x_sample.reshape(b_s * t_s, D_MODEL)
    outs_p = ([], [], [], [], [])
    outs_s = ([], [], [], [], [])
    for l in range(depth):
        (glu, sa, sz, xbc, dtv, q, lat, kr, sc, g, k, v) = _in_proj(y_p, wp, l, cos_p, sin_p, False, tm)
        ya, new_a = _conv_a(glu, sa, zero_a, wp, l, b_p, 256)
        yb, new_b, h_fin = _ssd_prompt(xbc, dtv, sz, zero_b, zero_h, wp, l, b_p)
        att = _attn_prompt(q, k, v, b_p, t_p)
        y_p = _out_proj(y_p, ya, yb, att, sc, g, wp, l, tm)
        for lst, val in zip(outs_p, (new_a, new_b, h_fin, lat.reshape(b_p, t_p, KV_LORA),
                                     kr.reshape(b_p, t_p, ROPE_DIM))):
            lst.append(val)

        (glu, sa, sz, xbc, dtv, q, lat, kr, sc, g, qabs) = _in_proj(y_s, wp, l, cos_s, sin_s, True, tm)
        ya, new_a = _conv_a(glu, sa, state_conv_a[l], wp, l, b_s, t_s, seqs=8)
        yb, new_b, h_fin = _ssd_sample(xbc, dtv, sz, state_conv_b[l], state_ssm, wp, l, b_s, t_s)
        qc = jnp.transpose(q[:, :, NOPE_DIM:QK_DIM], (1, 0, 2)).reshape(b_s, Q_ROWS, ROPE_DIM)
        kr_t = jnp.swapaxes(kr.reshape(b_s, t_s, ROPE_DIM), 1, 2)
        ctx = _attn_sample(page_table, qabs.reshape(b_s, Q_ROWS, KV_LORA), qc,
                           lat.reshape(b_s, t_s, KV_LORA), kr_t,
                           cache_latent, cache_krope_t, cos_k, sin_k, wp, l)
        att = _uv_proj(ctx.reshape(b_s * t_s, MLA_HEADS * KV_LORA), wp, l, tm)
        y_s = _out_proj(y_s, ya, yb, att, sc, g, wp, l, tm)
        for lst, val in zip(outs_s, (new_a, new_b, h_fin, lat.reshape(b_s, t_s, KV_LORA),
                                     kr.reshape(b_s, t_s, ROPE_DIM))):
            lst.append(val)

    stack = lambda lists: [jnp.stack(v, axis=0) for v in lists]
    return (y_p.reshape(b_p, t_p, D_MODEL), y_s.reshape(b_s, t_s, D_MODEL), *stack(outs_p), *stack(outs_s))
```

```python
import functools

import jax
import jax.numpy as jnp
from jax import lax
from jax.experimental import pallas as pl
from jax.experimental.pallas import tpu as pltpu

F32 = jnp.float32
BF16 = jnp.bfloat16
HIGHEST = lax.Precision.HIGHEST

D_MODEL = 1024
D_A = 512
CONV_A_WIDTH = 31
D_B = 1024
SSM_HEAD_DIM = 64
SSM_HEADS = 16
SSM_GROUPS = 2
SSM_STATE = 128
CONV_B_WIDTH = 4
CONV_B_DIM = D_B + 2 * SSM_GROUPS * SSM_STATE
MLA_HEADS = 8
NOPE_DIM = 64
ROPE_DIM = 32
HALF_ROPE = ROPE_DIM // 2
QK_DIM = NOPE_DIM + ROPE_DIM
V_DIM = 64
D_C = MLA_HEADS * V_DIM
Q_LORA = 384
KV_LORA = 256
ROPE_BASE = 10000.0
EPS = 1e-6
N_BRANCH = 3
PAGE = 128

LANES = 128
SUBLANES = 8
HEAD_BLOCK = LANES
QK_PAD = MLA_HEADS * HEAD_BLOCK
W1_COLS = 3 * D_A + D_B + CONV_B_DIM
OFF_CQ = 0
OFF_CKV = OFF_CQ + Q_LORA
OFF_CS = OFF_CKV + KV_LORA
OFF_MG = OFF_CS + D_C
OFF_KR = OFF_MG + N_BRANCH * D_MODEL
OFF_DT = OFF_KR + LANES
W2_COLS = OFF_DT + LANES

VMEM_LIMIT = 56 * 1024 * 1024


def _dot(a, b):
    return jnp.dot(a.astype(BF16), b.astype(BF16), preferred_element_type=F32)


def _dot_nt(a, b):
    return lax.dot_general(a.astype(BF16), b.astype(BF16), (((1,), (1,)), ((), ())),
                           preferred_element_type=F32)


def _sigmoid(x):
    return 1.0 / (1.0 + jnp.exp(-x))


def _silu(x):
    return x * _sigmoid(x)


def _rms(x, w):
    return x * lax.rsqrt(jnp.mean(x * x, axis=-1, keepdims=True) + EPS) * w


def _expand_heads(x, rows):
    lane = lax.broadcasted_iota(jnp.int32, (rows, LANES), 1)
    blocks = []
    for i in range(SSM_HEADS // 2):
        blocks.append(jnp.where(lane < SSM_HEAD_DIM, x[:, 2 * i:2 * i + 1], x[:, 2 * i + 1:2 * i + 2]))
    return jnp.concatenate(blocks, axis=1)


def _const_spec(shape, layer=None):
    if layer is None:
        return pl.BlockSpec(shape, lambda *_: (0,) * len(shape), pipeline_mode=pl.Buffered(1))
    return pl.BlockSpec((None,) + shape, lambda *_: (layer,) + (0,) * len(shape),
                        pipeline_mode=pl.Buffered(1))


def _params(n_axes):
    return pltpu.CompilerParams(dimension_semantics=("arbitrary",) * n_axes,
                                vmem_limit_bytes=VMEM_LIMIT)


def _norm_rope_block(val, w_row, cosb, sinb, lane):
    ss = jnp.sum(val * val, axis=-1, keepdims=True)
    vn = val * lax.rsqrt(ss * (1.0 / QK_DIM) + EPS) * w_row
    sw = jnp.where(lane < NOPE_DIM + HALF_ROPE,
                   pltpu.roll(vn, LANES - HALF_ROPE, 1), pltpu.roll(vn, HALF_ROPE, 1))
    return vn * cosb + sw * sinb


def _in_kernel(with_abs, x_ref, nw_ref, w1_ref, w2_ref, dtb_ref, qaw_ref, wuq_ref, kvw_ref, wuk_ref,
               wuv_ref, qnw_ref, knw_ref, cos_ref, sin_ref, *rest):
    if with_abs:
        knw_nope_ref, wabs_ref = rest[:2]
        rest = rest[2:]
    glu_ref, sa_ref, sz_ref, xbc_ref, dtv_ref, q_ref, lat_ref, kr_ref, sc_ref, g_ref = rest[:10]
    if with_abs:
        qabs_ref = rest[10]
    else:
        k_ref, v_ref = rest[10:12]
    tm = x_ref.shape[0]
    x = x_ref[...]
    hb = _rms(x, nw_ref[...]).astype(BF16)

    a_val = _dot(hb, w1_ref[:, 0:D_A])
    a_gate = _dot(hb, w1_ref[:, D_A:2 * D_A])
    glu_ref[...] = a_val * _sigmoid(a_gate)
    sa_ref[...] = _silu(_dot(hb, w1_ref[:, 2 * D_A:3 * D_A]))
    sz_ref[...] = _silu(_dot(hb, w1_ref[:, 3 * D_A:3 * D_A + D_B]))
    xbc_ref[...] = _dot(hb, w1_ref[:, 3 * D_A + D_B:W1_COLS])

    cq = _dot(hb, w2_ref[:, OFF_CQ:OFF_CKV])
    ckv = _dot(hb, w2_ref[:, OFF_CKV:OFF_CS])
    sc_ref[...] = _silu(_dot(hb, w2_ref[:, OFF_CS:OFF_MG]))
    g_ref[...] = _sigmoid(_dot(hb, w2_ref[:, OFF_MG:OFF_KR]))
    krp = _dot(hb, w2_ref[:, OFF_KR:OFF_DT])
    dtp = _dot(hb, w2_ref[:, OFF_DT:W2_COLS])
    kr_ref[...] = krp[:, NOPE_DIM:QK_DIM]
    dtx = dtp + dtb_ref[...]
    dtv_ref[...] = jnp.maximum(dtx, 0.0) + jnp.log1p(jnp.exp(-jnp.abs(dtx)))

    q = _dot(_rms(cq, qaw_ref[...]), wuq_ref[...])
    latn = _rms(ckv, kvw_ref[...])
    lat_ref[...] = latn
    latb = latn.astype(BF16)
    if not with_abs:
        kn = _dot(latb, wuk_ref[...])
        for p in range(MLA_HEADS // 2):
            v_ref[p] = _dot(latb, wuv_ref[:, p * LANES:(p + 1) * LANES]).astype(BF16)

    cosb = cos_ref[...]
    sinb = sin_ref[...]
    lane = lax.broadcasted_iota(jnp.int32, (tm, LANES), 1)
    scale = QK_DIM ** -0.5
    for h in range(MLA_HEADS):
        sl = slice(h * HEAD_BLOCK, (h + 1) * HEAD_BLOCK)
        qr = _norm_rope_block(q[:, sl], qnw_ref[...], cosb, sinb, lane) * scale
        q_ref[h] = qr.astype(BF16)
        if with_abs:
            qa = _dot(qr * knw_nope_ref[...], wabs_ref[sl, :])
            qabs_ref[:, h * KV_LORA:(h + 1) * KV_LORA] = qa.astype(BF16)
        else:
            kr = _norm_rope_block(kn[:, sl] + krp, knw_ref[...], cosb, sinb, lane)
            k_ref[h] = kr.astype(BF16)


def _in_proj(x, wp, layer, cos_tab, sin_tab, with_abs, tm=256):
    n = x.shape[0]
    assert n % tm == 0 and cos_tab.shape[0] % tm == 0
    nper = cos_tab.shape[0] // tm
    row = lambda w: pl.BlockSpec((tm, w), lambda i: (i, 0))
    tab = pl.BlockSpec((tm, LANES), lambda i: (i % nper, 0))
    in_specs = [
        row(D_MODEL),
        _const_spec((1, D_MODEL), layer),
        _const_spec((D_MODEL, W1_COLS), layer),
        _const_spec((D_MODEL, W2_COLS), layer),
        _const_spec((1, LANES), layer),
        _const_spec((1, Q_LORA), layer),
        _const_spec((Q_LORA, QK_PAD), layer),
        _const_spec((1, KV_LORA), layer),
        _const_spec((KV_LORA, QK_PAD), layer),
        _const_spec((KV_LORA, D_C), layer),
        _const_spec((1, LANES), layer),
        _const_spec((1, LANES), layer),
        tab, tab,
    ]
    args = [x, wp['norm_w'], wp['w1'], wp['w2'], wp['dt_bias'], wp['q_a_norm_w'], wp['w_uq'],
            wp['kv_a_norm_w'], wp['w_uk'], wp['w_uv'], wp['q_norm_w'], wp['k_norm_w'], cos_tab, sin_tab]
    heads = lambda nh: (jax.ShapeDtypeStruct((nh, n, LANES), BF16),
                        pl.BlockSpec((nh, tm, LANES), lambda i: (0, i, 0)))
    flat = lambda w, dt: (jax.ShapeDtypeStruct((n, w), dt), row(w))
    outs = [flat(D_A, F32), flat(D_A, F32), flat(D_B, F32), flat(CONV_B_DIM, F32), flat(LANES, F32),
            heads(MLA_HEADS), flat(KV_LORA, F32), flat(ROPE_DIM, F32), flat(D_C, F32),
            flat(N_BRANCH * D_MODEL, F32)]
    if with_abs:
        in_specs += [_const_spec((1, LANES), layer), _const_spec((QK_PAD, KV_LORA), layer)]
        args += [wp['k_norm_w_nope'], wp['w_uk_abs']]
        outs.append(flat(MLA_HEADS * KV_LORA, BF16))
    else:
        outs += [heads(MLA_HEADS), heads(MLA_HEADS // 2)]
    return pl.pallas_call(
        functools.partial(_in_kernel, with_abs),
        out_shape=[o[0] for o in outs],
        grid=(n // tm,),
        in_specs=in_specs,
        out_specs=[o[1] for o in outs],
        compiler_params=_params(1),
    )(*args)


HIST_A = CONV_A_WIDTH - 1
HIST_A_PAD = 32


def _conv_a_kernel(glu_ref, sa_ref, hist_ref, *refs):
    seqs = hist_ref.shape[0]
    tt = glu_ref.shape[0] // seqs
    for s in range(seqs):
        _conv_a_one(s, tt, glu_ref, sa_ref, hist_ref, *refs)


def _conv_a_one(s, tt, glu_ref, sa_ref, hist_ref, cw_ref, cb_ref, lw_ref, lb_ref, ya_ref, newa_ref, bufs):
    t = pl.program_id(1)
    lo = HIST_A_PAD - HIST_A
    rows = slice(s * tt, (s + 1) * tt)
    buf = bufs.at[s]

    @pl.when(t == 0)
    def _():
        buf[lo:HIST_A_PAD, :] = hist_ref[s]
        buf[HIST_A_PAD + tt:HIST_A_PAD + tt + SUBLANES, :] = jnp.zeros((SUBLANES, D_A), F32)

    buf[HIST_A_PAD:HIST_A_PAD + tt, :] = glu_ref[rows, :]
    acc = jnp.broadcast_to(cb_ref[...], (tt, D_A))
    for r in range(SUBLANES):
        part = None
        for a in range((lo + CONV_A_WIDTH - 1) // SUBLANES + 1):
            k = SUBLANES * a + r - lo
            if 0 <= k < CONV_A_WIDTH:
                term = cw_ref[k:k + 1, :] * buf[SUBLANES * a:SUBLANES * a + tt + SUBLANES, :]
                part = term if part is None else part + term
        acc = acc + part[r:r + tt, :]
    xc = acc - jnp.mean(acc, axis=-1, keepdims=True)
    ln = xc * lax.rsqrt(jnp.mean(xc * xc, axis=-1, keepdims=True) + EPS) * lw_ref[...] + lb_ref[...]
    ya_ref[rows, :] = _silu(ln) * sa_ref[rows, :]
    tail = buf[tt + lo:tt + HIST_A_PAD, :]

    @pl.when(t == pl.num_programs(1) - 1)
    def _():
        newa_ref[s] = tail

    buf[lo:HIST_A_PAD, :] = tail


def _conv_a(glu, sa, hist, wp, layer, batch, tt, seqs=1):
    n = glu.shape[0]
    nt = n // (batch * tt)
    assert batch % seqs == 0 and (seqs == 1 or nt == 1)
    row = pl.BlockSpec((seqs * tt, D_A), lambda b, t: (b * nt + t, 0))
    histb = pl.BlockSpec((seqs, HIST_A, D_A), lambda b, t: (b, 0, 0))
    return pl.pallas_call(
        _conv_a_kernel,
        out_shape=[jax.ShapeDtypeStruct((n, D_A), F32),
                   jax.ShapeDtypeStruct((batch, HIST_A, D_A), F32)],
        grid=(batch // seqs, nt),
        in_specs=[row, row, histb,
                  _const_spec((CONV_A_WIDTH, D_A), layer),
                  _const_spec((1, D_A), layer), _const_spec((1, D_A), layer),
                  _const_spec((1, D_A), layer)],
        out_specs=[row, histb],
        scratch_shapes=[pltpu.VMEM((seqs, HIST_A_PAD + tt + SUBLANES, D_A), F32)],
        compiler_params=_params(2),
    )(glu, sa, hist, wp['conv_a_w'], wp['conv_a_b'], wp['ln_a_w'], wp['ln_a_b'])


HIST_B = CONV_B_WIDTH - 1
HIST_B_PAD = 8
GROUP_COLS = D_B // SSM_GROUPS
HEADS_PER_GROUP = SSM_HEADS // SSM_GROUPS


def _conv_b(cbuf, xbc_new, hist_loader, cw_ref, cb_ref, q, first):
    lo = HIST_B_PAD - HIST_B

    if first is None:
        cbuf[lo:HIST_B_PAD, :] = hist_loader()
    else:
        @pl.when(first)
        def _():
            cbuf[lo:HIST_B_PAD, :] = hist_loader()

    cbuf[HIST_B_PAD:HIST_B_PAD + q, :] = xbc_new
    acc = jnp.broadcast_to(cb_ref[...], (q, CONV_B_DIM))
    for k in range(CONV_B_WIDTH):
        acc = acc + cw_ref[k:k + 1, :] * cbuf[lo + k:lo + k + q, :]
    tail = cbuf[q + lo:q + HIST_B_PAD, :]
    return _silu(acc), tail


def _ssd_p_kernel(xbc_ref, hist_ref, cw_ref, cb_ref, dtv_ref, a_ref, dsk_ref, sz_ref, nbw_ref, h0_ref,
                  yb_ref, newb_ref, hfin_ref, cbuf, ht):
    c = pl.program_id(1)
    last = c == pl.num_programs(1) - 1
    q = xbc_ref.shape[0]
    xbc, tail = _conv_b(cbuf, xbc_ref[...], lambda: hist_ref[0], cw_ref, cb_ref, q, c == 0)

    @pl.when(c == 0)
    def _():
        ht[...] = h0_ref[0].reshape(SSM_HEADS * SSM_HEAD_DIM, SSM_STATE).T

    @pl.when(last)
    def _():
        newb_ref[0] = tail

    cbuf[HIST_B_PAD - HIST_B:HIST_B_PAD, :] = tail

    xs = xbc[:, :D_B]
    bs = xbc[:, D_B:D_B + SSM_GROUPS * SSM_STATE]
    cs = xbc[:, D_B + SSM_GROUPS * SSM_STATE:]
    dt = dtv_ref[...]
    adt = dt * a_ref[...]
    row = lax.broadcasted_iota(jnp.int32, (q, q), 0)
    col = lax.broadcasted_iota(jnp.int32, (q, q), 1)
    tril = row >= col
    acum = jnp.dot(tril.astype(F32), adt, precision=HIGHEST, preferred_element_type=F32)
    acum_t = acum.T
    dt_t = dt.T
    alast = acum[q - 1:q, :]
    dec_end = jnp.exp(alast - acum) * dt
    xsb = xs.astype(BF16)
    xw = (xs * _expand_heads(dec_end, q)).astype(BF16)
    chunk_decay = _expand_heads(jnp.exp(alast), 1)
    lane4 = lax.broadcasted_iota(jnp.int32, (q, 4 * SSM_HEAD_DIM), 1) // SSM_HEAD_DIM

    y_diag = []
    y_off = []
    for g in range(SSM_GROUPS):
        bsg = bs[:, g * SSM_STATE:(g + 1) * SSM_STATE]
        csg = cs[:, g * SSM_STATE:(g + 1) * SSM_STATE]
        cb = _dot_nt(csg, bsg)
        for quad in range(HEADS_PER_GROUP // 4):
            ws = []
            rhs = []
            qd = g * (HEADS_PER_GROUP // 4) + quad
            xblk = xsb[:, qd * 4 * SSM_HEAD_DIM:(qd + 1) * 4 * SSM_HEAD_DIM]
            for hh in range(4):
                h = qd * 4 + hh
                seg = acum[:, h:h + 1] - acum_t[h:h + 1, :]
                lm = jnp.exp(jnp.where(tril, seg, -jnp.inf))
                ws.append((cb * lm * dt_t[h:h + 1, :]).astype(BF16))
                rhs.append(jnp.where(lane4 == hh, xblk, jnp.zeros_like(xblk)))
            y_diag.append(jnp.dot(jnp.concatenate(ws, axis=1), jnp.concatenate(rhs, axis=0),
                                  preferred_element_type=F32))
        gs = slice(g * GROUP_COLS, (g + 1) * GROUP_COLS)
        htg = ht[:, gs]
        y_off.append(_dot(csg, htg))
        ht[:, gs] = htg * chunk_decay[:, gs] + _dot(bsg.T, xw[:, gs])

    y = (jnp.concatenate(y_diag, axis=1)
         + jnp.concatenate(y_off, axis=1) * _expand_heads(jnp.exp(acum), q)
         + dsk_ref[...] * xs)
    yb_ref[...] = _rms(y * sz_ref[...], nbw_ref[...])

    @pl.when(last)
    def _():
        hfin_ref[0] = ht[...].T.reshape(SSM_HEADS, SSM_HEAD_DIM, SSM_STATE)


def _ssd_prompt(xbc, dtv, sz, hist, h0, wp, layer, batch, q=128):
    n = xbc.shape[0]
    nc = n // (batch * q)
    row = lambda w: pl.BlockSpec((q, w), lambda b, c: (b * nc + c, 0))
    state = pl.BlockSpec((1, SSM_HEADS, SSM_HEAD_DIM, SSM_STATE), lambda b, c: (b, 0, 0, 0))
    histb = pl.BlockSpec((1, HIST_B, CONV_B_DIM), lambda b, c: (b, 0, 0))
    return pl.pallas_call(
        _ssd_p_kernel,
        out_shape=[jax.ShapeDtypeStruct((n, D_B), F32),
                   jax.ShapeDtypeStruct((batch, HIST_B, CONV_B_DIM), F32),
                   jax.ShapeDtypeStruct((batch, SSM_HEADS, SSM_HEAD_DIM, SSM_STATE), F32)],
        grid=(batch, nc),
        in_specs=[row(CONV_B_DIM), histb,
                  _const_spec((CONV_B_WIDTH, CONV_B_DIM), layer), _const_spec((1, CONV_B_DIM), layer),
                  row(LANES), _const_spec((1, LANES), layer), _const_spec((1, D_B), layer),
                  row(D_B), _const_spec((1, D_B), layer), state],
        out_specs=[row(D_B), histb, state],
        scratch_shapes=[pltpu.VMEM((HIST_B_PAD + q, CONV_B_DIM), F32),
                        pltpu.VMEM((SSM_STATE, D_B), F32)],
        compiler_params=_params(2),
    )(xbc, hist, wp['conv_b_w'], wp['conv_b_b'], dtv, wp['a_neg'], wp['d_skip'], sz, wp['norm_b_w'], h0)


def _ssd_s_kernel(q, xbc_ref, *refs):
    for s in range(xbc_ref.shape[0] // q):
        _ssd_s_one(s, q, xbc_ref, *refs)


def _ssd_s_one(s, q, xbc_ref, hist_ref, cw_ref, cb_ref, dtv_ref, a_ref, dsk_ref, sz_ref, nbw_ref, h0_ref,
               yb_ref, newb_ref, hfin_ref, cbuf):
    assert q * SSM_HEADS == LANES
    rows = slice(s * q, (s + 1) * q)
    xbc, tail = _conv_b(cbuf.at[s], xbc_ref[rows, :], lambda: hist_ref[s], cw_ref, cb_ref, q, None)
    newb_ref[s] = tail
    xs = xbc[:, :D_B]
    bs = xbc[:, D_B:D_B + SSM_GROUPS * SSM_STATE]
    cs = xbc[:, D_B + SSM_GROUPS * SSM_STATE:]
    dt = dtv_ref[rows, :]
    adt = dt * a_ref[...]
    row = lax.broadcasted_iota(jnp.int32, (q, LANES), 0)
    lane = lax.broadcasted_iota(jnp.int32, (q, LANES), 1)
    tril8 = (row >= lane).astype(F32)[:, :q]
    acum = jnp.dot(tril8, adt, precision=HIGHEST, preferred_element_type=F32)
    er = lax.broadcasted_iota(jnp.int32, (LANES, LANES), 0)
    ec = lax.broadcasted_iota(jnp.int32, (LANES, LANES), 1)
    rep = (er == ec // q).astype(F32)
    a_col = jnp.dot(acum, rep, precision=HIGHEST, preferred_element_type=F32)
    d_col = jnp.dot(dt, rep, precision=HIGHEST, preferred_element_type=F32)
    sel = row == lane % q
    a_row = jnp.sum(jnp.where(sel, a_col, 0.0), axis=0, keepdims=True)
    d_row = jnp.sum(jnp.where(sel, d_col, 0.0), axis=0, keepdims=True)
    lm = jnp.exp(jnp.where(row >= lane % q, a_col - a_row, -jnp.inf))
    br = lax.broadcasted_iota(jnp.int32, (LANES, SSM_GROUPS * SSM_STATE), 0)
    bc = lax.broadcasted_iota(jnp.int32, (LANES, SSM_GROUPS * SSM_STATE), 1)
    bs_t = jnp.where(br // (q * HEADS_PER_GROUP) == bc // SSM_STATE,
                     jnp.concatenate([bs] * SSM_HEADS, axis=0), 0.0)
    cb_flat = _dot_nt(cs, bs_t)
    w = (cb_flat * lm * d_row).astype(BF16)
    xr = lax.broadcasted_iota(jnp.int32, (LANES, D_B), 0)
    xc = lax.broadcasted_iota(jnp.int32, (LANES, D_B), 1)
    rhs = jnp.where(xr // q == xc // SSM_HEAD_DIM, jnp.concatenate([xs] * SSM_HEADS, axis=0), 0.0)
    y_diag = jnp.dot(w, rhs.astype(BF16), preferred_element_type=F32)

    h0 = h0_ref[s].reshape(SSM_HEADS * SSM_HEAD_DIM, SSM_STATE)
    alast = acum[q - 1:q, :]
    ea_last = jnp.exp(alast)
    dec_end = jnp.exp(alast - acum) * dt
    xw = (xs * _expand_heads(dec_end, q)).astype(BF16)
    ir = lax.broadcasted_iota(jnp.int32, (GROUP_COLS, GROUP_COLS), 0)
    ic = lax.broadcasted_iota(jnp.int32, (GROUP_COLS, GROUP_COLS), 1)
    eye = (ir == ic).astype(BF16)
    y_off = []
    for g in range(SSM_GROUPS):
        gs = slice(g * GROUP_COLS, (g + 1) * GROUP_COLS)
        bsg = bs[:, g * SSM_STATE:(g + 1) * SSM_STATE]
        csg = cs[:, g * SSM_STATE:(g + 1) * SSM_STATE]
        y_off.append(_dot_nt(csg, h0[gs, :]))
        xw_t = _dot_nt(eye, xw[:, gs])
        st = _dot(xw_t, bsg)
        for hh in range(HEADS_PER_GROUP):
            h = g * HEADS_PER_GROUP + hh
            rs = slice(hh * SSM_HEAD_DIM, (hh + 1) * SSM_HEAD_DIM)
            hfin_ref[s, h] = h0_ref[s, h] * ea_last[:, h:h + 1] + st[rs, :]
    y = (y_diag + jnp.concatenate(y_off, axis=1) * _expand_heads(jnp.exp(acum), q)
         + dsk_ref[...] * xs)
    yb_ref[rows, :] = _rms(y * sz_ref[rows, :], nbw_ref[...])


def _ssd_sample(xbc, dtv, sz, hist, h0, wp, layer, batch, q, seqs=4):
    n = xbc.shape[0]
    assert batch % seqs == 0
    row = lambda w: pl.BlockSpec((seqs * q, w), lambda b: (b, 0))
    state = pl.BlockSpec((seqs, SSM_HEADS, SSM_HEAD_DIM, SSM_STATE), lambda b: (b, 0, 0, 0))
    state_in = pl.BlockSpec((None, seqs, SSM_HEADS, SSM_HEAD_DIM, SSM_STATE),
                            lambda b: (layer, b, 0, 0, 0))
    histb = pl.BlockSpec((seqs, HIST_B, CONV_B_DIM), lambda b: (b, 0, 0))
    return pl.pallas_call(
        functools.partial(_ssd_s_kernel, q),
        out_shape=[jax.ShapeDtypeStruct((n, D_B), F32),
                   jax.ShapeDtypeStruct((batch, HIST_B, CONV_B_DIM), F32),
                   jax.ShapeDtypeStruct((batch, SSM_HEADS, SSM_HEAD_DIM, SSM_STATE), F32)],
        grid=(batch // seqs,),
        in_specs=[row(CONV_B_DIM), histb,
                  _const_spec((CONV_B_WIDTH, CONV_B_DIM), layer), _const_spec((1, CONV_B_DIM), layer),
                  row(LANES), _const_spec((1, LANES), layer), _const_spec((1, D_B), layer),
                  row(D_B), _const_spec((1, D_B), layer), state_in],
        out_specs=[row(D_B), histb, state],
        scratch_shapes=[pltpu.VMEM((seqs, HIST_B_PAD + q, CONV_B_DIM), F32)],
        compiler_params=_params(1),
    )(xbc, hist, wp['conv_b_w'], wp['conv_b_b'], dtv, wp['a_neg'], wp['d_skip'], sz, wp['norm_b_w'], h0)


def _attn_p_kernel(tq, q_ref, k_ref, v_ref, o_ref):
    seq = q_ref.shape[1]
    row = lax.broadcasted_iota(jnp.int32, (tq, tq), 0)
    col = lax.broadcasted_iota(jnp.int32, (tq, tq), 1)
    causal = row >= col
    lane = lax.broadcasted_iota(jnp.int32, (tq, LANES), 1)
    for i in range(seq // tq):
        lo = i * tq
        rows = slice(lo, lo + tq)

        def pair(p, carry, lo=lo, rows=rows):
            outs = []
            for e in range(2):
                h = 2 * p + e
                qh = q_ref[h, rows, :]
                s_d = jnp.where(causal, _dot_nt(qh, k_ref[h, rows, :]), -jnp.inf)
                m = jnp.max(s_d, axis=-1, keepdims=True)
                if lo:
                    s_p = _dot_nt(qh, k_ref[h, 0:lo, :])
                    m = jnp.maximum(m, jnp.max(s_p, axis=-1, keepdims=True))
                p_d = jnp.exp(s_d - m)
                l = jnp.sum(p_d, axis=-1, keepdims=True)
                pv = jnp.dot(p_d.astype(BF16), v_ref[p, rows, :], preferred_element_type=F32)
                if lo:
                    p_p = jnp.exp(s_p - m)
                    l = l + jnp.sum(p_p, axis=-1, keepdims=True)
                    pv = pv + jnp.dot(p_p.astype(BF16), v_ref[p, 0:lo, :], preferred_element_type=F32)
                outs.append(pv * (1.0 / l))
            o_ref[p, rows, :] = jnp.where(lane < V_DIM, outs[0], outs[1])
            return carry

        lax.fori_loop(0, MLA_HEADS // 2, pair, 0)


def _attn_prompt(q, k, v, batch, seq, tq=256):
    n = q.shape[1]
    head_rows = lambda nh: pl.BlockSpec((nh, seq, LANES), lambda b: (0, b, 0))
    return pl.pallas_call(
        functools.partial(_attn_p_kernel, tq),
        out_shape=jax.ShapeDtypeStruct((MLA_HEADS // 2, n, LANES), F32),
        grid=(batch,),
        in_specs=[head_rows(MLA_HEADS), head_rows(MLA_HEADS), head_rows(MLA_HEADS // 2)],
        out_specs=head_rows(MLA_HEADS // 2),
        compiler_params=_params(1),
    )(q, k, v)


PAGES_PER_CHUNK = 16
CHUNK = PAGES_PER_CHUNK * PAGE
Q_ROWS = 64
KNT_ROWS = MLA_HEADS * NOPE_DIM
LHS_ROWS = KNT_ROWS + Q_ROWS


def _attn_s_kernel(layer, n_chunks, past, pt_ref, qn_ref, qc_ref, wukt_ref, knwr_ref, cos_ref, sin_ref,
                   latn_ref, krn_ref, clat_hbm, ckr_hbm, ctx_ref,
                   latbuf, krbuf, sems, lhs_scr, latb_scr, s_scr, newlat, newkr):
    b = pl.program_id(0)
    nb = pl.num_programs(0)
    t_new = latn_ref.shape[1]

    n_land = latbuf.shape[0]

    def copies(bb, cc):
        slot = cc % n_land
        out = []
        for k in range(PAGES_PER_CHUNK):
            page = pt_ref[bb, cc * PAGES_PER_CHUNK + k]
            keys = pl.ds(k * PAGE, PAGE)
            out.append(pltpu.make_async_copy(clat_hbm.at[layer, page], latbuf.at[slot, keys],
                                             sems.at[0, slot]))
            out.append(pltpu.make_async_copy(ckr_hbm.at[layer, page], krbuf.at[slot, :, keys],
                                             sems.at[1, slot]))
        return out

    def scores(bb, latb, krt, pos0):
        ck = latb.shape[0]
        lhs_scr[KNT_ROWS:LHS_ROWS, :] = qn_ref[bb]
        big = _dot_nt(lhs_scr[...], latb)
        knt = big[0:KNT_ROWS, :]
        ssq = jnp.sum((knt * knt).reshape(MLA_HEADS, NOPE_DIM, ck), axis=1)
        ssr = jnp.sum(krt * krt, axis=0, keepdims=True)
        r = lax.rsqrt((ssq + ssr) * (1.0 / QK_DIM) + EPS)
        x = krt * knwr_ref[...]
        x1 = x[0:HALF_ROPE, :]
        x2 = x[HALF_ROPE:ROPE_DIM, :]
        cos = cos_ref[:, pos0:pos0 + ck]
        sin = sin_ref[:, pos0:pos0 + ck]
        kro = jnp.concatenate([x1 * cos - x2 * sin, x2 * cos + x1 * sin], axis=0)
        s = big[KNT_ROWS:LHS_ROWS, :] + _dot(qc_ref[bb], kro)
        return s * jnp.concatenate([r] * (Q_ROWS // MLA_HEADS), axis=0)

    def stage1(bb, cc):
        latb = latbuf[cc % n_land].astype(BF16)
        latb_scr[cc % 2] = latb
        s_scr[cc % 2] = scores(bb, latb, krbuf[cc % n_land], cc * CHUNK)

    def stage2(s, latb, carry, mask):
        m, l, acc = carry
        if mask is not None:
            s = jnp.where(mask, s, -1e30)
        m_new = jnp.maximum(m, jnp.max(s, axis=-1, keepdims=True))
        alpha = jnp.exp(m - m_new)
        pr = jnp.exp(s - m_new)
        if mask is not None:
            pr = jnp.where(mask, pr, 0.0)
        l = alpha * l + jnp.sum(pr, axis=-1, keepdims=True)
        acc = alpha * acc + jnp.dot(pr.astype(BF16), latb, preferred_element_type=F32)
        return m_new, l, acc

    @pl.when(b == 0)
    def _():
        lhs_scr[0:KNT_ROWS, :] = wukt_ref[...]
        for c0 in range(n_land - 1):
            for cp in copies(0, c0):
                cp.start()
        for cp in copies(0, 0):
            cp.wait()
        stage1(0, 0)
        newlat[...] = jnp.zeros_like(newlat)
        newkr[...] = jnp.zeros_like(newkr)

    nxt = jnp.minimum(b + 1, nb - 1)
    carry = (jnp.full((Q_ROWS, 1), -1e30, F32), jnp.zeros((Q_ROWS, 1), F32),
             jnp.zeros((Q_ROWS, KV_LORA), F32))
    def issue(c, ahead, wait):
        wraps = c + ahead >= n_chunks

        def run():
            for cp in copies(b + 1 if wraps else b, (c + ahead) % n_chunks):
                cp.wait() if wait else cp.start()

        if wraps:
            pl.when(b + 1 < nb)(run)
        else:
            run()

    for c in range(n_chunks):
        issue(c, n_land - 1, False)
        issue(c, 1, True)
        wraps = c + 1 >= n_chunks
        stage1(nxt if wraps else b, (c + 1) % n_chunks)
        carry = stage2(s_scr[c % 2], latb_scr[c % 2], carry, None)

    newlat[0:t_new, :] = latn_ref[b]
    newkr[:, 0:t_new] = krn_ref[b]
    qi = lax.broadcasted_iota(jnp.int32, (Q_ROWS, PAGE), 0) // MLA_HEADS
    kj = lax.broadcasted_iota(jnp.int32, (Q_ROWS, PAGE), 1)
    latb = newlat[...].astype(BF16)
    _, l, acc = stage2(scores(b, latb, newkr[...], past), latb, carry, kj <= qi)
    ctx_ref[0] = acc * (1.0 / l)


def _attn_sample(page_table, qn, qc, latn, krn_t, cache_latent, cache_krope_t, cos_t, sin_t, wp, layer):
    batch, n_pages = page_table.shape
    n_chunks = n_pages // PAGES_PER_CHUNK
    assert n_pages % PAGES_PER_CHUNK == 0 and n_chunks % 2 == 0
    n_land = 4 if n_chunks % 4 == 0 else 2
    past = n_pages * PAGE
    per_seq = lambda shape: pl.BlockSpec((1,) + shape, lambda b, pt: (b, 0, 0))
    grid_spec = pltpu.PrefetchScalarGridSpec(
        num_scalar_prefetch=1,
        grid=(batch,),
        in_specs=[_const_spec(qn.shape), _const_spec(qc.shape),
                  _const_spec((KNT_ROWS, KV_LORA), layer),
                  _const_spec((ROPE_DIM, 1), layer),
                  _const_spec(cos_t.shape), _const_spec(sin_t.shape),
                  _const_spec(latn.shape), _const_spec(krn_t.shape),
                  pl.BlockSpec(memory_space=pl.ANY), pl.BlockSpec(memory_space=pl.ANY)],
        out_specs=per_seq((Q_ROWS, KV_LORA)),
        scratch_shapes=[pltpu.VMEM((n_land, CHUNK, KV_LORA), F32),
                        pltpu.VMEM((n_land, ROPE_DIM, CHUNK), F32),
                        pltpu.SemaphoreType.DMA((2, n_land)),
                        pltpu.VMEM((LHS_ROWS, KV_LORA), BF16),
                        pltpu.VMEM((2, CHUNK, KV_LORA), BF16),
                        pltpu.VMEM((2, Q_ROWS, CHUNK), F32),
                        pltpu.VMEM((PAGE, KV_LORA), F32),
                        pltpu.VMEM((ROPE_DIM, PAGE), F32)],
    )
    return pl.pallas_call(
        functools.partial(_attn_s_kernel, layer, n_chunks, past),
        out_shape=jax.ShapeDtypeStruct((batch, Q_ROWS, KV_LORA), F32),
        grid_spec=grid_spec,
        compiler_params=_params(1),
    )(page_table, qn, qc, wp['w_uk_t'], wp['k_norm_w_rope'], cos_t, sin_t, latn, krn_t,
      cache_latent, cache_krope_t)


def _uv_kernel(ctx_ref, wv_ref, att_ref):
    for p in range(MLA_HEADS // 2):
        acc = None
        for e in range(2):
            h = 2 * p + e
            part = _dot(ctx_ref[:, h * KV_LORA:(h + 1) * KV_LORA], wv_ref[h])
            acc = part if acc is None else acc + part
        att_ref[p] = acc


def _uv_proj(ctx, wp, layer, tm=256):
    n = ctx.shape[0]
    tm = min(tm, n)
    return pl.pallas_call(
        _uv_kernel,
        out_shape=jax.ShapeDtypeStruct((MLA_HEADS // 2, n, LANES), F32),
        grid=(n // tm,),
        in_specs=[pl.BlockSpec((tm, MLA_HEADS * KV_LORA), lambda i: (i, 0)),
                  _const_spec((MLA_HEADS, KV_LORA, LANES), layer)],
        out_specs=pl.BlockSpec((MLA_HEADS // 2, tm, LANES), lambda i: (0, i, 0)),
        compiler_params=_params(1),
    )(ctx, wp['w_uv_pair'])


def _out_kernel(x_ref, ya_ref, yb_ref, att_ref, sc_ref, g_ref, wa_ref, wb_ref, wc_ref, wo_ref, y_ref):
    br_a = _dot(ya_ref[...], wa_ref[...])
    br_b = _dot(yb_ref[...], wb_ref[...])
    att = jnp.concatenate([att_ref[p] for p in range(MLA_HEADS // 2)], axis=1)
    br_c = _dot(att * sc_ref[...], wc_ref[...])
    merged = (g_ref[:, 0:D_MODEL] * br_a + g_ref[:, D_MODEL:2 * D_MODEL] * br_b
              + g_ref[:, 2 * D_MODEL:3 * D_MODEL] * br_c)
    y_ref[...] = x_ref[...] + _dot(merged, wo_ref[...])


def _out_proj(x, ya, yb, att, sc, g, wp, layer, tm=256):
    n = x.shape[0]
    row = lambda w: pl.BlockSpec((tm, w), lambda i: (i, 0))
    return pl.pallas_call(
        _out_kernel,
        out_shape=jax.ShapeDtypeStruct((n, D_MODEL), F32),
        grid=(n // tm,),
        in_specs=[row(D_MODEL), row(D_A), row(D_B),
                  pl.BlockSpec((MLA_HEADS // 2, tm, LANES), lambda i: (0, i, 0)),
                  row(D_C), row(N_BRANCH * D_MODEL),
                  _const_spec((D_A, D_MODEL), layer), _const_spec((D_B, D_MODEL), layer),
                  _const_spec((D_C, D_MODEL), layer), _const_spec((D_MODEL, D_MODEL), layer)],
        out_specs=row(D_MODEL),
        compiler_params=_params(1),
    )(x, ya, yb, att, sc, g, wp['w_a_out'], wp['w_b_out'], wp['w_c_out'], wp['w_out'])


def _pad_lanes(row, width=LANES):
    return jnp.pad(row, ((0, 0), (0, 0), (0, width - row.shape[-1])))


def _prepare_weights(w_in, dt_bias, a_log, d_skip, w_uq, w_uk, w_uv, q_norm_w, k_norm_w, others):
    depth = w_in.shape[0]
    o = 0
    cuts = {}
    for name, width in (('a', 3 * D_A), ('z', D_B), ('xbc', CONV_B_DIM), ('dt', SSM_HEADS), ('cq', Q_LORA),
                        ('ckv', KV_LORA), ('kr', ROPE_DIM), ('cs', D_C), ('mg', N_BRANCH * D_MODEL)):
        cuts[name] = (o, o + width)
        o += width
    col = lambda name: w_in[:, :, cuts[name][0]:cuts[name][1]]
    zeros = lambda width: jnp.zeros((depth, D_MODEL, width), w_in.dtype)
    w1 = w_in[:, :, 0:W1_COLS].astype(BF16)
    w2 = jnp.concatenate([col('cq'), col('ckv'), col('cs'), col('mg'),
                          zeros(NOPE_DIM), col('kr'), zeros(LANES - QK_DIM),
                          col('dt'), zeros(LANES - SSM_HEADS)], axis=-1).astype(BF16)
    head_pad = lambda w: jnp.pad(w, ((0, 0), (0, 0), (0, 0), (0, HEAD_BLOCK - w.shape[-1])))
    wuq = head_pad(w_uq.reshape(depth, Q_LORA, MLA_HEADS, QK_DIM)).reshape(depth, Q_LORA, QK_PAD)
    wuk = head_pad(w_uk).reshape(depth, KV_LORA, QK_PAD)
    wuk_abs = jnp.transpose(head_pad(w_uk), (0, 2, 3, 1)).reshape(depth, QK_PAD, KV_LORA)
    wuk_t = jnp.transpose(w_uk, (0, 2, 3, 1)).reshape(depth, MLA_HEADS * NOPE_DIM, KV_LORA)
    wuv = w_uv.reshape(depth, KV_LORA, D_C)
    wv_h = jnp.transpose(w_uv, (0, 2, 1, 3))
    left = jnp.pad(wv_h, ((0, 0), (0, 0), (0, 0), (0, V_DIM)))
    right = jnp.pad(wv_h, ((0, 0), (0, 0), (0, 0), (V_DIM, 0)))
    even = (jnp.arange(MLA_HEADS) % 2 == 0)[None, :, None, None]
    wuv_pair = jnp.where(even, left, right)
    wp = dict(others)
    wp.update(
        w1=w1, w2=w2,
        dt_bias=_pad_lanes(dt_bias[:, None, :]),
        a_neg=_pad_lanes(-jnp.exp(a_log.astype(F32))[:, None, :]),
        d_skip=jnp.repeat(d_skip, SSM_HEAD_DIM, axis=-1)[:, None, :],
        w_uq=wuq.astype(BF16), w_uk=wuk.astype(BF16), w_uv=wuv.astype(BF16),
        w_uk_abs=wuk_abs.astype(BF16), w_uk_t=wuk_t.astype(BF16), w_uv_pair=wuv_pair.astype(BF16),
        q_norm_w=_pad_lanes(q_norm_w[:, None, :]), k_norm_w=_pad_lanes(k_norm_w[:, None, :]),
        k_norm_w_nope=_pad_lanes(k_norm_w[:, None, :NOPE_DIM]),
        k_norm_w_rope=k_norm_w[:, NOPE_DIM:, None],
    )
    return wp


def _rope_angles(pos):
    inv = jnp.power(ROPE_BASE, -jnp.arange(HALF_ROPE, dtype=F32) / HALF_ROPE)
    ang = pos.astype(F32)[:, None] * inv[None, :]
    return jnp.cos(ang), jnp.sin(ang)


def _head_block_tables(pos):
    cos, sin = _rope_angles(pos)
    n = pos.shape[0]
    cos_tab = jnp.concatenate([jnp.ones((n, NOPE_DIM), F32), cos, cos,
                               jnp.ones((n, LANES - QK_DIM), F32)], axis=1)
    sin_tab = jnp.concatenate([jnp.zeros((n, NOPE_DIM), F32), -sin, sin,
                               jnp.zeros((n, LANES - QK_DIM), F32)], axis=1)
    return cos_tab, sin_tab


def kernel(x_prompt, x_sample, state_conv_a, state_conv_b, state_ssm, cache_latent, cache_krope, page_table,
           norm_w, w_in, conv_a_w, conv_a_b, ln_a_w, ln_a_b, w_a_out, conv_b_w, conv_b_b, dt_bias, a_log,
           d_skip, norm_b_w, w_b_out, q_a_norm_w, w_uq, kv_a_norm_w, w_uk, w_uv, q_norm_w, k_norm_w,
           w_c_out, w_out):
    depth = w_in.shape[0]
    b_p, t_p, _ = x_prompt.shape
    b_s, t_s, _ = x_sample.shape
    n_pages = page_table.shape[1]
    past = n_pages * PAGE
    tm = 256
    assert (b_s * t_s) % tm == 0 and tm % t_s == 0 and t_p % tm == 0

    row3 = lambda w: w[:, None, :]
    others = dict(
        norm_w=row3(norm_w), conv_a_w=conv_a_w, conv_a_b=row3(conv_a_b), ln_a_w=row3(ln_a_w),
        ln_a_b=row3(ln_a_b), conv_b_w=conv_b_w, conv_b_b=row3(conv_b_b), norm_b_w=row3(norm_b_w),
        q_a_norm_w=row3(q_a_norm_w), kv_a_norm_w=row3(kv_a_norm_w),
        w_a_out=w_a_out.astype(BF16), w_b_out=w_b_out.astype(BF16), w_c_out=w_c_out.astype(BF16),
        w_out=w_out.astype(BF16))
    wp = _prepare_weights(w_in, dt_bias, a_log, d_skip, w_uq, w_uk, w_uv, q_norm_w, k_norm_w, others)

    cos_p, sin_p = _head_block_tables(jnp.arange(t_p, dtype=jnp.int32))
    pos_s = past + jnp.arange(t_s, dtype=jnp.int32)
    cos_s, sin_s = _head_block_tables(jnp.tile(pos_s, tm // t_s))
    cos_k, sin_k = (tab.T for tab in _rope_angles(jnp.arange(past + PAGE, dtype=jnp.int32)))
    cache_krope_t = jnp.swapaxes(cache_krope, 2, 3)

    zero_a = jnp.zeros((b_p, HIST_A, D_A), F32)
    zero_b = jnp.zeros((b_p, HIST_B, CONV_B_DIM), F32)
    zero_h = jnp.zeros((b_p, SSM_HEADS, SSM_HEAD_DIM, SSM_STATE), F32)

    y_p = x_prompt.reshape(b_p * t_p, D_MODEL)
    y_s = x_sample.reshape(b_s * t_s, D_MODEL)
    outs_p = ([], [], [], [], [])
    outs_s = ([], [], [], [], [])
    for l in range(depth):
        (glu, sa, sz, xbc, dtv, q, lat, kr, sc, g, k, v) = _in_proj(y_p, wp, l, cos_p, sin_p, False, tm)
        ya, new_a = _conv_a(glu, sa, zero_a, wp, l, b_p, 256)
        yb, new_b, h_fin = _ssd_prompt(xbc, dtv, sz, zero_b, zero_h, wp, l, b_p)
        att = _attn_prompt(q, k, v, b_p, t_p)
        y_p = _out_proj(y_p, ya, yb, att, sc, g, wp, l, tm)
        for lst, val in zip(outs_p, (new_a, new_b, h_fin, lat.reshape(b_p, t_p, KV_LORA),
                                     kr.reshape(b_p, t_p, ROPE_DIM))):
            lst.append(val)

        (glu, sa, sz, xbc, dtv, q, lat, kr, sc, g, qabs) = _in_proj(y_s, wp, l, cos_s, sin_s, True, tm)
        ya, new_a = _conv_a(glu, sa, state_conv_a[l], wp, l, b_s, t_s, seqs=8)
        yb, new_b, h_fin = _ssd_sample(xbc, dtv, sz, state_conv_b[l], state_ssm, wp, l, b_s, t_s)
        qc = jnp.transpose(q[:, :, NOPE_DIM:QK_DIM], (1, 0, 2)).reshape(b_s, Q_ROWS, ROPE_DIM)
        kr_t = jnp.swapaxes(kr.reshape(b_s, t_s, ROPE_DIM), 1, 2)
        ctx = _attn_sample(page_table, qabs.reshape(b_s, Q_ROWS, KV_LORA), qc,
                           lat.reshape(b_s, t_s, KV_LORA), kr_t,
                           cache_latent, cache_krope_t, cos_k, sin_k, wp, l)
        att = _uv_proj(ctx.reshape(b_s * t_s, MLA_HEADS * KV_LORA), wp, l, tm)
        y_s = _out_proj(y_s, ya, yb, att, sc, g, wp, l, tm)
        for lst, val in zip(outs_s, (new_a, new_b, h_fin, lat.reshape(b_s, t_s, KV_LORA),
                                     kr.reshape(b_s, t_s, ROPE_DIM))):
            lst.append(val)

    stack = lambda lists: [jnp.stack(v, axis=0) for v in lists]
    return (y_p.reshape(b_p, t_p, D_MODEL), y_s.reshape(b_s, t_s, D_MODEL), *stack(outs_p), *stack(outs_s))
```
